```python
import jax, jax.numpy as jnp
from jax import lax
import numpy as np

D_MODEL = 1024
BATCH = 8
SEQ = 4096
DEPTH = 2

CHUNK = 64
N_EVEN = (DEPTH + 1) // 2
N_ODD = DEPTH // 2
EPS = 1e-6

POOL_WINDOWS = (2, 4, 8, 16)
POOL_GROUP = D_MODEL // 8
POOL_WIDTH = POOL_GROUP * len(POOL_WINDOWS)
SC_WIDTH = D_MODEL // 2
SC_GROUPS = 8
CONV_WIDTH = 3
AB_IN = POOL_WIDTH + 3 * SC_WIDTH
AB_OUT = POOL_WIDTH + SC_WIDTH
GLA_HEADS = 4
GLA_DK = D_MODEL // 2
GLA_DV = D_MODEL
GLA_HK = GLA_DK // GLA_HEADS
GLA_HV = GLA_DV // GLA_HEADS
GLA_RANK = 16
GLA_TAU = 16.0
GLA_IN = 2 * GLA_DK + 2 * GLA_DV + GLA_RANK
D_FF = 2816

kernel_name = "hybrid_pool_shortconv_gla_convffn"


def rmsnorm(x, g):
    xf = x.astype(jnp.float32)
    y = xf * lax.rsqrt(jnp.mean(xf * xf, axis=-1, keepdims=True) + EPS)
    return (y * g.astype(jnp.float32)).astype(x.dtype)


def causal_dwconv(u, w, b):
    k = w.shape[0]
    s = u.shape[1]
    up = jnp.pad(u, ((0, 0), (k - 1, 0), (0, 0)))
    y = b
    for i in range(k):
        y = y + up[:, i:i + s] * w[i]
    return y


def pool_mixer(u, w, b, scale):
    bsz, s, _ = u.shape
    t = jnp.arange(1, s + 1, dtype=jnp.float32)[None, :, None]
    uf = u.astype(jnp.float32)
    outs = []
    for gi, win in enumerate(POOL_WINDOWS):
        ug = uf[..., gi * POOL_GROUP:(gi + 1) * POOL_GROUP]
        c = jnp.cumsum(ug, axis=1)
        c_lag = jnp.pad(c, ((0, 0), (win, 0), (0, 0)))[:, :s]
        mean = (c - c_lag) / jnp.minimum(t, float(win))
        outs.append(mean - ug)
    p = jnp.stack(outs, axis=2).astype(u.dtype)
    y = jnp.einsum('bsgc,gcd->bsgd', p, w).reshape(bsz, s, POOL_WIDTH) + b
    return y * scale


def gla_mixer(h, w_g2, b_g, norm_g):
    bsz, s, _ = h.shape
    f32 = jnp.float32
    q, k, v, r, gl = jnp.split(h, [GLA_DK, 2 * GLA_DK, 2 * GLA_DK + GLA_DV, 2 * GLA_DK + 2 * GLA_DV], axis=-1)
    g = jax.nn.log_sigmoid((gl @ w_g2 + b_g).astype(f32)) / GLA_TAU
    n = s // CHUNK

    def heads(t, hd):
        return t.astype(f32).reshape(bsz, n, CHUNK, GLA_HEADS, hd).transpose(0, 3, 1, 2, 4)

    q = heads(q, GLA_HK) * (GLA_HK ** -0.5)
    k = heads(k, GLA_HK)
    v = heads(v, GLA_HV)
    bcum = jnp.cumsum(heads(g, GLA_HK), axis=3)
    b_last = bcum[:, :, :, -1:]
    q_in = q * jnp.exp(bcum)
    k_in = k * jnp.exp(-bcum)
    mask = jnp.tril(jnp.ones((CHUNK, CHUNK), dtype=bool))
    att = jnp.where(mask, jnp.einsum('bhnlk,bhnmk->bhnlm', q_in, k_in), 0.0)
    o_intra = jnp.einsum('bhnlm,bhnmv->bhnlv', att, v)
    kv = jnp.einsum('bhnlk,bhnlv->bhnkv', k * jnp.exp(b_last - bcum), v)
    decay = jnp.exp(b_last[:, :, :, 0])

    def step(state, inp):
        kv_n, d_n = inp
        return d_n[..., None] * state + kv_n, state

    init = jnp.zeros((bsz, GLA_HEADS, GLA_HK, GLA_HV), f32)
    _, states = lax.scan(step, init, (jnp.moveaxis(kv, 2, 0), jnp.moveaxis(decay, 2, 0)))
    states = jnp.moveaxis(states, 0, 2)
    o = o_intra + jnp.einsum('bhnlk,bhnkv->bhnlv', q_in, states)
    o = o.transpose(0, 2, 3, 1, 4).reshape(bsz, s, GLA_HEADS, GLA_HV)
    o = o * lax.rsqrt(jnp.mean(o * o, axis=-1, keepdims=True) + EPS) * norm_g.astype(f32)
    o = o.reshape(bsz, s, GLA_DV) * jax.nn.silu(r.astype(f32))
    return o.astype(h.dtype)


def conv_ffn(h, w_up, conv_w, conv_b, w_down):
    u, v = jnp.split(h @ w_up, 2, axis=-1)
    u = causal_dwconv(u, conv_w, conv_b)
    return (jax.nn.gelu(u, approximate=False) * v) @ w_down


def setup_inputs(seed: int = 0) -> dict:
    key = jax.random.key(seed)
    ks = jax.random.split(key, 24)
    nrm = jax.random.normal
    f = jnp.float32
    return {
        "x": nrm(ks[0], (BATCH, SEQ, D_MODEL), f),
        "mix_norm": 1.0 + 0.02 * nrm(ks[1], (DEPTH, D_MODEL), f),
        "ffn_norm": 1.0 + 0.02 * nrm(ks[2], (DEPTH, D_MODEL), f),
        "ab_w_in": nrm(ks[3], (N_EVEN, D_MODEL, AB_IN), f) * D_MODEL ** -0.5,
        "pool_w": nrm(ks[4], (N_EVEN, len(POOL_WINDOWS), POOL_GROUP, POOL_GROUP), f) * POOL_GROUP ** -0.5,
        "pool_b": 0.02 * nrm(ks[5], (N_EVEN, POOL_WIDTH), f),
        "pool_scale": 1.0 + 0.02 * nrm(ks[6], (N_EVEN, POOL_WIDTH), f),
        "sc_conv_w": nrm(ks[7], (N_EVEN, CONV_WIDTH, SC_WIDTH), f) * CONV_WIDTH ** -0.5,
        "sc_conv_b": 0.02 * nrm(ks[8], (N_EVEN, SC_WIDTH), f),
        "ab_w_out": nrm(ks[9], (N_EVEN, AB_OUT, D_MODEL), f) * AB_OUT ** -0.5,
        "gla_w_in": nrm(ks[10], (N_ODD, D_MODEL, GLA_IN), f) * D_MODEL ** -0.5,
        "gla_w_g2": nrm(ks[11], (N_ODD, GLA_RANK, GLA_DK), f) * GLA_RANK ** -0.5,
        "gla_b_g": 0.1 * nrm(ks[12], (N_ODD, GLA_DK), f),
        "gla_norm": 1.0 + 0.02 * nrm(ks[13], (N_ODD, GLA_HV), f),
        "gla_w_out": nrm(ks[14], (N_ODD, GLA_DV, D_MODEL), f) * GLA_DV ** -0.5,
        "ffn_w_up": nrm(ks[15], (DEPTH, D_MODEL, 2 * D_FF), f) * D_MODEL ** -0.5,
        "ffn_conv_w": nrm(ks[16], (DEPTH, CONV_WIDTH, D_FF), f) * CONV_WIDTH ** -0.5,
        "ffn_conv_b": 0.02 * nrm(ks[17], (DEPTH, D_FF), f),
        "ffn_w_down": nrm(ks[18], (DEPTH, D_FF, D_MODEL), f) * D_FF ** -0.5,
        "final_norm": 1.0 + 0.02 * nrm(ks[19], (D_MODEL,), f),
    }


def reference(x, mix_norm, ffn_norm, ab_w_in, pool_w, pool_b, pool_scale, sc_conv_w, sc_conv_b,
              ab_w_out, gla_w_in, gla_w_g2, gla_b_g, gla_norm, gla_w_out, ffn_w_up, ffn_conv_w,
              ffn_conv_b, ffn_w_down, final_norm):
    for l in range(DEPTH):
        hn = rmsnorm(x, mix_norm[l])
        i = l // 2
        if l % 2 == 0:
            h = hn @ ab_w_in[i]
            pu, sb, sc, sx = jnp.split(h, [POOL_WIDTH, POOL_WIDTH + SC_WIDTH, POOL_WIDTH + 2 * SC_WIDTH], axis=-1)
            ya = pool_mixer(pu, pool_w[i], pool_b[i], pool_scale[i])
            yb = sb * causal_dwconv(sc * sx, sc_conv_w[i], sc_conv_b[i])
            y = jnp.concatenate([ya, yb], axis=-1) @ ab_w_out[i]
        else:
            y = gla_mixer(hn @ gla_w_in[i], gla_w_g2[i], gla_b_g[i], gla_norm[i]) @ gla_w_out[i]
        x = x + y
        x = x + conv_ffn(rmsnorm(x, ffn_norm[l]), ffn_w_up[l], ffn_conv_w[l], ffn_conv_b[l], ffn_w_down[l])
    return rmsnorm(x, final_norm)
```

```python
import functools
import math

import jax
import jax.numpy as jnp
from jax import lax
from jax.experimental import pallas as pl
from jax.experimental.pallas import tpu as pltpu

F32 = jnp.float32
BF16 = jnp.bfloat16

EPS = 1e-6
CHUNK = 64
POOL_WINDOWS = (2, 4, 8, 16)
POOL_GROUP = 128
CONV_WIDTH = 3
GLA_HEADS = 4
GLA_RANK = 16
GLA_TAU = 16.0

LANES = 128
SUBLANES = 8
MXU_COLS = 256
POOL_HIST = 16
CONV_HIST = SUBLANES
FF_CHUNK = MXU_COLS
VMEM_LIMIT_BYTES = 56 * 1024 * 1024

SQRT_HALF = math.sqrt(0.5)


def _rms(x, g):
    ms = jnp.mean(x * x, axis=-1, keepdims=True)
    return x * lax.rsqrt(ms + EPS) * g


def _shifted(hist, cur, shift):
    ext = jnp.concatenate([hist, cur], axis=0)
    return pltpu.roll(ext, shift, axis=0)[hist.shape[0]:]


def _causal_conv3(hist, u, w, b):
    u1 = _shifted(hist, u, 1)
    u2 = _shifted(hist, u, 2)
    return b + w[0:1] * u2 + w[1:2] * u1 + w[2:3] * u


def _dot(a, b):
    return jnp.dot(a, b, preferred_element_type=F32)


def _ab_kernel(x_ref, g_ref, w_in_ref, pw_ref, pb_ref, ps_ref, cw_ref, cb_ref, w_out_ref,
               o_ref, pool_hist, conv_hist, *, tm, pool_width, sc_width):
    s = pl.program_id(1)

    @pl.when(s == 0)
    def _():
        pool_hist[...] = jnp.zeros_like(pool_hist)
        conv_hist[...] = jnp.zeros_like(conv_hist)

    x = x_ref[...]
    hn = _rms(x, g_ref[...]).astype(BF16)
    h = _dot(hn, w_in_ref[...])
    pu = h[:, :pool_width]
    sb = h[:, pool_width:pool_width + sc_width]
    sc = h[:, pool_width + sc_width:pool_width + 2 * sc_width]
    sx = h[:, pool_width + 2 * sc_width:]

    frame = lax.broadcasted_iota(jnp.int32, (tm, POOL_GROUP), 0) + (s * tm + 1)
    ext = jnp.concatenate([pool_hist[...], pu], axis=0)
    ya = []
    for gi, win in enumerate(POOL_WINDOWS):
        lo = gi * POOL_GROUP
        e = ext[:, lo:lo + POOL_GROUP]
        sh = 1
        while sh < win:
            e = e + pltpu.roll(e, sh, axis=0)
            sh *= 2
        ug = pu[:, lo:lo + POOL_GROUP]
        mean = e[POOL_HIST:] / jnp.minimum(frame, win).astype(F32)
        ya.append(_dot((mean - ug).astype(BF16), pw_ref[gi]))
    ya = (jnp.concatenate(ya, axis=-1) + pb_ref[...]) * ps_ref[...]

    z = sc * sx
    yb = sb * _causal_conv3(conv_hist[...], z, cw_ref[...], cb_ref[...])

    y = jnp.concatenate([ya, yb], axis=-1).astype(BF16)
    o_ref[...] = x + _dot(y, w_out_ref[...])

    pool_hist[...] = pu[tm - POOL_HIST:]
    conv_hist[...] = z[tm - CONV_HIST:]


def _gla_kernel(x_ref, g_ref, wq_ref, wk_ref, wv_ref, wr_ref, wgl_ref, wg2_ref, bg_ref, ng_ref,
                w_out_ref, o_ref, state_ref, hn_s, q_s, k_s, v_s, gate_s, o_s, *, tm, hk, hv):
    s = pl.program_id(1)

    @pl.when(s == 0)
    def _():
        state_ref[...] = jnp.zeros_like(state_ref)

    x = x_ref[...]
    hn = _rms(x, g_ref[...]).astype(BF16)
    hn_s[...] = hn
    q_s[...] = _dot(hn, wq_ref[...]) * (hk ** -0.5)
    k_s[...] = _dot(hn, wk_ref[...])
    v_s[...] = _dot(hn, wv_ref[...])
    gl = _dot(hn, wgl_ref[...])
    gpre = _dot(gl.astype(BF16), wg2_ref[...]) + bg_ref[...]
    gate_s[...] = (jnp.minimum(gpre, 0.0) - jnp.log1p(jnp.exp(-jnp.abs(gpre)))) * (1.0 / GLA_TAU)

    ri = lax.broadcasted_iota(jnp.int32, (CHUNK, CHUNK), 0)
    ci = lax.broadcasted_iota(jnp.int32, (CHUNK, CHUNK), 1)
    causal = ri >= ci
    tri = causal.astype(BF16)
    ng = ng_ref[...]

    def chunk_body(c, carry):
        r0 = pl.multiple_of(c * CHUNK, CHUNK)
        rows = pl.ds(r0, CHUNK)
        gc = gate_s[rows, :]
        g1 = gc.astype(BF16)
        r1 = gc - g1.astype(F32)
        g2 = r1.astype(BF16)
        g3 = (r1 - g2.astype(F32)).astype(BF16)
        bcum = _dot(tri, g1) + _dot(tri, g2) + _dot(tri, g3)
        blast = bcum[CHUNK - 1:CHUNK]
        kc = k_s[rows, :]
        q_in = (q_s[rows, :] * jnp.exp(bcum)).astype(BF16)
        k_in = (kc * jnp.exp(-bcum)).astype(BF16)
        k_out = (kc * jnp.exp(blast - bcum)).astype(BF16)
        decay = jnp.exp(blast)
        for hd in range(GLA_HEADS):
            ks = slice(hd * hk, (hd + 1) * hk)
            vs = slice(hd * hv, (hd + 1) * hv)
            qh = q_in[:, ks]
            vh = v_s[rows, vs].astype(BF16)
            att = lax.dot_general(qh, k_in[:, ks], (((1,), (1,)), ((), ())),
                                  preferred_element_type=F32)
            att = jnp.where(causal, att, 0.0).astype(BF16)
            st = state_ref[hd]
            o = _dot(att, vh) + lax.dot_general(qh, st.astype(BF16), (((1,), (1,)), ((), ())),
                                                 preferred_element_type=F32)
            kv_t = lax.dot_general(vh, k_out[:, ks], (((0,), (0,)), ((), ())),
                                   preferred_element_type=F32)
            state_ref[hd] = st * decay[:, ks] + kv_t
            o = o * lax.rsqrt(jnp.mean(o * o, axis=-1, keepdims=True) + EPS) * ng
            o_s[rows, vs] = o
        return carry

    lax.fori_loop(0, tm // CHUNK, chunk_body, 0)

    r = _dot(hn_s[...], wr_ref[...])
    y = (o_s[...] * (r * jax.nn.sigmoid(r))).astype(BF16)
    o_ref[...] = x + _dot(y, w_out_ref[...])


def _ffn_kernel(x_ref, g_ref, wup_ref, cw_ref, cb_ref, wdn_ref, fg_ref, o_ref, conv_hist, hn_s,
                *, tm, n_chunks, final_norm):
    s = pl.program_id(1)

    @pl.when(s == 0)
    def _():
        conv_hist[...] = jnp.zeros_like(conv_hist)

    x = x_ref[...]
    hn_s[...] = _rms(x, g_ref[...]).astype(BF16)
    o_ref[...] = x

    def chunk_body(c, carry):
        uv = _dot(hn_s[...], wup_ref[c])
        u = uv[:, :FF_CHUNK]
        v = uv[:, FF_CHUNK:]
        uc = _causal_conv3(conv_hist[c], u, cw_ref[c], cb_ref[c])
        a = 0.5 * uc * (1.0 + lax.erf(uc * SQRT_HALF)) * v
        o_ref[...] += _dot(a.astype(BF16), wdn_ref[c])
        conv_hist[c] = u[tm - CONV_HIST:]
        return carry

    lax.fori_loop(0, n_chunks, chunk_body, 0)

    if final_norm:
        o_ref[...] = _rms(o_ref[...], fg_ref[...])


def _const_spec(shape):
    nd = len(shape)
    return pl.BlockSpec(shape, lambda b, s: (0,) * nd, pipeline_mode=pl.Buffered(1))


def _tile_spec(tm, d):
    return pl.BlockSpec((None, tm, d), lambda b, s: (b, s, 0))


def _call(body, x, consts, scratch, tm):
    bsz, seq, d = x.shape
    return pl.pallas_call(
        body,
        grid=(bsz, seq // tm),
        in_specs=[_tile_spec(tm, d)] + [_const_spec(c.shape) for c in consts],
        out_specs=_tile_spec(tm, d),
        out_shape=jax.ShapeDtypeStruct(x.shape, x.dtype),
        scratch_shapes=scratch,
        compiler_params=pltpu.CompilerParams(
            dimension_semantics=("arbitrary", "arbitrary"),
            vmem_limit_bytes=VMEM_LIMIT_BYTES),
    )(x, *consts)


def _row(v):
    return v.reshape(1, -1).astype(F32)


def _ab_layer(x, norm_g, w_in, pool_w, pool_b, pool_scale, conv_w, conv_b, w_out, tm):
    pool_width = pool_b.shape[0]
    sc_width = conv_b.shape[0]
    consts = [_row(norm_g), w_in.astype(BF16), pool_w.astype(BF16), _row(pool_b), _row(pool_scale),
              conv_w.astype(F32), _row(conv_b), w_out.astype(BF16)]
    scratch = [pltpu.VMEM((POOL_HIST, pool_width), F32), pltpu.VMEM((CONV_HIST, sc_width), F32)]
    body = functools.partial(_ab_kernel, tm=tm, pool_width=pool_width, sc_width=sc_width)
    return _call(body, x, consts, scratch, tm)


def _gla_layer(x, norm_g, w_in, w_g2, b_g, head_norm, w_out, tm):
    d = x.shape[-1]
    dk = w_g2.shape[1]
    dv = w_out.shape[0]
    hk = dk // GLA_HEADS
    hv = dv // GLA_HEADS
    wq = w_in[:, :dk]
    wk = w_in[:, dk:2 * dk]
    wv = w_in[:, 2 * dk:2 * dk + dv]
    wr = w_in[:, 2 * dk + dv:2 * dk + 2 * dv]
    wgl = jnp.pad(w_in[:, 2 * dk + 2 * dv:], ((0, 0), (0, LANES - GLA_RANK)))
    wg2 = jnp.pad(w_g2, ((0, LANES - GLA_RANK), (0, 0)))
    consts = [_row(norm_g), wq.astype(BF16), wk.astype(BF16), wv.astype(BF16), wr.astype(BF16),
              wgl.astype(BF16), wg2.astype(BF16), _row(b_g), _row(head_norm), w_out.astype(BF16)]
    scratch = [pltpu.VMEM((GLA_HEADS, hv, hk), F32),
               pltpu.VMEM((tm, d), BF16),
               pltpu.VMEM((tm, dk), F32),
               pltpu.VMEM((tm, dk), F32),
               pltpu.VMEM((tm, dv), F32),
               pltpu.VMEM((tm, dk), F32),
               pltpu.VMEM((tm, dv), F32)]
    body = functools.partial(_gla_kernel, tm=tm, hk=hk, hv=hv)
    return _call(body, x, consts, scratch, tm)


def _ffn_layer(x, norm_g, w_up, conv_w, conv_b, w_down, final_g, tm, final_norm):
    d = x.shape[-1]
    d_ff = w_down.shape[0]
    n_chunks = d_ff // FF_CHUNK
    assert n_chunks * FF_CHUNK == d_ff

    def chunked(w):
        return w.reshape(w.shape[0], n_chunks, FF_CHUNK).transpose(1, 0, 2)

    wup = jnp.concatenate([chunked(w_up[:, :d_ff]), chunked(w_up[:, d_ff:])], axis=-1).astype(BF16)
    consts = [_row(norm_g), wup, chunked(conv_w).astype(F32), chunked(_row(conv_b)),
              w_down.reshape(n_chunks, FF_CHUNK, d).astype(BF16), _row(final_g)]
    scratch = [pltpu.VMEM((n_chunks, CONV_HIST, FF_CHUNK), F32), pltpu.VMEM((tm, d), BF16)]
    body = functools.partial(_ffn_kernel, tm=tm, n_chunks=n_chunks, final_norm=final_norm)
    return _call(body, x, consts, scratch, tm)


def kernel(x, mix_norm, ffn_norm, ab_w_in, pool_w, pool_b, pool_scale, sc_conv_w, sc_conv_b,
           ab_w_out, gla_w_in, gla_w_g2, gla_b_g, gla_norm, gla_w_out, ffn_w_up, ffn_conv_w,
           ffn_conv_b, ffn_w_down, final_norm):
    depth = mix_norm.shape[0]
    tm = 512
    assert x.shape[1] % tm == 0 and tm % CHUNK == 0
    for l in range(depth):
        i = l // 2
        if l % 2 == 0:
            x = _ab_layer(x, mix_norm[l], ab_w_in[i], pool_w[i], pool_b[i], pool_scale[i],
                          sc_conv_w[i], sc_conv_b[i], ab_w_out[i], tm)
        else:
            x = _gla_layer(x, mix_norm[l], gla_w_in[i], gla_w_g2[i], gla_b_g[i], gla_norm[i],
                           gla_w_out[i], tm)
        x = _ffn_layer(x, ffn_norm[l], ffn_w_up[l], ffn_conv_w[l], ffn_conv_b[l], ffn_w_down[l],
                       final_norm, tm, final_norm=(l == depth - 1))
    return x
```

```python
import functools
import math

import jax
import jax.numpy as jnp
from jax import lax
from jax.experimental import pallas as pl
from jax.experimental.pallas import tpu as pltpu

F32 = jnp.float32
BF16 = jnp.bfloat16

EPS = 1e-6
CHUNK = 64
POOL_WINDOWS = (2, 4, 8, 16)
POOL_GROUP = 128
CONV_WIDTH = 3
GLA_HEADS = 4
GLA_RANK = 16
GLA_TAU = 16.0

LANES = 128
SUBLANES = 8
MXU_COLS = 256
POOL_HIST = 16
CONV_HIST = SUBLANES
FF_CHUNK = 2 * MXU_COLS
VMEM_LIMIT_BYTES = 56 * 1024 * 1024

SQRT_HALF = math.sqrt(0.5)


def _rms(x, g):
    ms = jnp.mean(x * x, axis=-1, keepdims=True)
    return x * lax.rsqrt(ms + EPS) * g


def _shifted(hist, cur, shift):
    ext = jnp.concatenate([hist, cur], axis=0)
    return pltpu.roll(ext, shift, axis=0)[hist.shape[0]:]


def _causal_conv3(hist, u, w, b):
    u1 = _shifted(hist, u, 1)
    u2 = _shifted(hist, u, 2)
    return b + w[0:1] * u2 + w[1:2] * u1 + w[2:3] * u


def _dot(a, b):
    return jnp.dot(a, b, preferred_element_type=F32)


def _ab_kernel(x_ref, g_ref, w_in_ref, pw_ref, pb_ref, ps_ref, cw_ref, cb_ref, w_out_ref,
               o_ref, pool_hist, conv_hist, *, tm, pool_width, sc_width):
    s = pl.program_id(1)

    @pl.when(s == 0)
    def _():
        pool_hist[...] = jnp.zeros_like(pool_hist)
        conv_hist[...] = jnp.zeros_like(conv_hist)

    x = x_ref[...]
    hn = _rms(x, g_ref[...]).astype(BF16)
    h = _dot(hn, w_in_ref[...])
    pu = h[:, :pool_width]
    sb = h[:, pool_width:pool_width + sc_width]
    sc = h[:, pool_width + sc_width:pool_width + 2 * sc_width]
    sx = h[:, pool_width + 2 * sc_width:]

    frame = lax.broadcasted_iota(jnp.int32, (tm, POOL_GROUP), 0) + (s * tm + 1)
    ext = jnp.concatenate([pool_hist[...], pu], axis=0)
    ya = []
    for gi, win in enumerate(POOL_WINDOWS):
        lo = gi * POOL_GROUP
        e = ext[:, lo:lo + POOL_GROUP]
        sh = 1
        while sh < win:
            e = e + pltpu.roll(e, sh, axis=0)
            sh *= 2
        ug = pu[:, lo:lo + POOL_GROUP]
        mean = e[POOL_HIST:] / jnp.minimum(frame, win).astype(F32)
        ya.append(_dot((mean - ug).astype(BF16), pw_ref[gi]))
    ya = (jnp.concatenate(ya, axis=-1) + pb_ref[...]) * ps_ref[...]

    z = sc * sx
    yb = sb * _causal_conv3(conv_hist[...], z, cw_ref[...], cb_ref[...])

    y = jnp.concatenate([ya, yb], axis=-1).astype(BF16)
    o_ref[...] = x + _dot(y, w_out_ref[...])

    pool_hist[...] = pu[tm - POOL_HIST:]
    conv_hist[...] = z[tm - CONV_HIST:]


def _gla_kernel(x_ref, g_ref, wq_ref, wk_ref, wv_ref, wr_ref, wgl_ref, wg2_ref, bg_ref, ng_ref,
                w_out_ref, o_ref, state_ref, hn_s, q_s, k_s, v_s, gate_s, o_s, *, tm, hk, hv):
    s = pl.program_id(1)

    @pl.when(s == 0)
    def _():
        state_ref[...] = jnp.zeros_like(state_ref)

    x = x_ref[...]
    hn = _rms(x, g_ref[...]).astype(BF16)
    hn_s[...] = hn
    q_s[...] = _dot(hn, wq_ref[...]) * (hk ** -0.5)
    k_s[...] = _dot(hn, wk_ref[...])
    v_s[...] = _dot(hn, wv_ref[...])
    gl = _dot(hn, wgl_ref[...])
    gpre = _dot(gl.astype(BF16), wg2_ref[...]) + bg_ref[...]
    gate_s[...] = (jnp.minimum(gpre, 0.0) - jnp.log1p(jnp.exp(-jnp.abs(gpre)))) * (1.0 / GLA_TAU)

    ri = lax.broadcasted_iota(jnp.int32, (CHUNK, CHUNK), 0)
    ci = lax.broadcasted_iota(jnp.int32, (CHUNK, CHUNK), 1)
    causal = ri >= ci
    tri = causal.astype(BF16)
    ng = ng_ref[...]

    for c in range(tm // CHUNK):
        rows = slice(c * CHUNK, (c + 1) * CHUNK)
        gc = gate_s[rows, :]
        g1 = gc.astype(BF16)
        r1 = gc - g1.astype(F32)
        g2 = r1.astype(BF16)
        g3 = (r1 - g2.astype(F32)).astype(BF16)
        bcum = _dot(tri, g1) + _dot(tri, g2) + _dot(tri, g3)
        blast = bcum[CHUNK - 1:CHUNK]
        kc = k_s[rows, :]
        q_in = (q_s[rows, :] * jnp.exp(bcum)).astype(BF16)
        k_in = (kc * jnp.exp(-bcum)).astype(BF16)
        k_out = (kc * jnp.exp(blast - bcum)).astype(BF16)
        decay = jnp.exp(blast)
        for hd in range(GLA_HEADS):
            ks = slice(hd * hk, (hd + 1) * hk)
            vs = slice(hd * hv, (hd + 1) * hv)
            qh = q_in[:, ks]
            vh = v_s[rows, vs].astype(BF16)
            att = lax.dot_general(qh, k_in[:, ks], (((1,), (1,)), ((), ())),
                                  preferred_element_type=F32)
            att = jnp.where(causal, att, 0.0).astype(BF16)
            st = state_ref[hd]
            o = _dot(att, vh) + lax.dot_general(qh, st.astype(BF16), (((1,), (1,)), ((), ())),
                                                 preferred_element_type=F32)
            kv_t = lax.dot_general(vh, k_out[:, ks], (((0,), (0,)), ((), ())),
                                   preferred_element_type=F32)
            state_ref[hd] = st * decay[:, ks] + kv_t
            o = o * lax.rsqrt(jnp.mean(o * o, axis=-1, keepdims=True) + EPS) * ng
            o_s[rows, vs] = o

    r = _dot(hn_s[...], wr_ref[...])
    y = (o_s[...] * (r * jax.nn.sigmoid(r))).astype(BF16)
    o_ref[...] = x + _dot(y, w_out_ref[...])


def _ffn_kernel(x_ref, g_ref, wup_ref, cw_ref, cb_ref, wdn_ref, fg_ref, o_ref, conv_hist, hn_s, a_s,
                *, tm, d_ff, final_norm):
    s = pl.program_id(1)

    @pl.when(s == 0)
    def _():
        conv_hist[...] = jnp.zeros_like(conv_hist)

    x = x_ref[...]
    hn_s[...] = _rms(x, g_ref[...]).astype(BF16)

    for c0 in range(0, d_ff, FF_CHUNK):
        cols = slice(c0, min(c0 + FF_CHUNK, d_ff))
        gcols = slice(d_ff + cols.start, d_ff + cols.stop)
        u = _dot(hn_s[...], wup_ref[:, cols])
        v = _dot(hn_s[...], wup_ref[:, gcols])
        uc = _causal_conv3(conv_hist[:, cols], u, cw_ref[:, cols], cb_ref[:, cols])
        a_s[:, cols] = (0.5 * uc * (1.0 + lax.erf(uc * SQRT_HALF)) * v).astype(BF16)
        conv_hist[:, cols] = u[tm - CONV_HIST:]

    out = x + _dot(a_s[...], wdn_ref[...])
    if final_norm:
        out = _rms(out, fg_ref[...])
    o_ref[...] = out


def _const_spec(shape):
    nd = len(shape)
    return pl.BlockSpec(shape, lambda b, s: (0,) * nd, pipeline_mode=pl.Buffered(1))


def _tile_spec(tm, d):
    return pl.BlockSpec((None, tm, d), lambda b, s: (b, s, 0))


def _call(body, x, consts, scratch, tm, name):
    bsz, seq, d = x.shape
    return pl.pallas_call(
        body,
        name=name,
        grid=(bsz, seq // tm),
        in_specs=[_tile_spec(tm, d)] + [_const_spec(c.shape) for c in consts],
        out_specs=_tile_spec(tm, d),
        out_shape=jax.ShapeDtypeStruct(x.shape, x.dtype),
        scratch_shapes=scratch,
        compiler_params=pltpu.CompilerParams(
            dimension_semantics=("arbitrary", "arbitrary"),
            vmem_limit_bytes=VMEM_LIMIT_BYTES),
    )(x, *consts)


def _row(v):
    return v.reshape(1, -1).astype(F32)


def _ab_layer(x, norm_g, w_in, pool_w, pool_b, pool_scale, conv_w, conv_b, w_out, tm):
    pool_width = pool_b.shape[0]
    sc_width = conv_b.shape[0]
    consts = [_row(norm_g), w_in.astype(BF16), pool_w.astype(BF16), _row(pool_b), _row(pool_scale),
              conv_w.astype(F32), _row(conv_b), w_out.astype(BF16)]
    scratch = [pltpu.VMEM((POOL_HIST, pool_width), F32), pltpu.VMEM((CONV_HIST, sc_width), F32)]
    body = functools.partial(_ab_kernel, tm=tm, pool_width=pool_width, sc_width=sc_width)
    return _call(body, x, consts, scratch, tm, "pool_conv_mixer")


def _gla_layer(x, norm_g, w_in, w_g2, b_g, head_norm, w_out, tm):
    d = x.shape[-1]
    dk = w_g2.shape[1]
    dv = w_out.shape[0]
    hk = dk // GLA_HEADS
    hv = dv // GLA_HEADS
    wq = w_in[:, :dk]
    wk = w_in[:, dk:2 * dk]
    wv = w_in[:, 2 * dk:2 * dk + dv]
    wr = w_in[:, 2 * dk + dv:2 * dk + 2 * dv]
    wgl = jnp.pad(w_in[:, 2 * dk + 2 * dv:], ((0, 0), (0, LANES - GLA_RANK)))
    wg2 = jnp.pad(w_g2, ((0, LANES - GLA_RANK), (0, 0)))
    consts = [_row(norm_g), wq.astype(BF16), wk.astype(BF16), wv.astype(BF16), wr.astype(BF16),
              wgl.astype(BF16), wg2.astype(BF16), _row(b_g), _row(head_norm), w_out.astype(BF16)]
    scratch = [pltpu.VMEM((GLA_HEADS, hv, hk), F32),
               pltpu.VMEM((tm, d), BF16),
               pltpu.VMEM((tm, dk), F32),
               pltpu.VMEM((tm, dk), F32),
               pltpu.VMEM((tm, dv), F32),
               pltpu.VMEM((tm, dk), F32),
               pltpu.VMEM((tm, dv), F32)]
    body = functools.partial(_gla_kernel, tm=tm, hk=hk, hv=hv)
    return _call(body, x, consts, scratch, tm, "gla_mixer")


def _ffn_layer(x, norm_g, w_up, conv_w, conv_b, w_down, final_g, tm, final_norm):
    d = x.shape[-1]
    d_ff = w_down.shape[0]
    assert d_ff % LANES == 0
    consts = [_row(norm_g), w_up.astype(BF16), conv_w.astype(F32), _row(conv_b),
              w_down.astype(BF16), _row(final_g)]
    scratch = [pltpu.VMEM((CONV_HIST, d_ff), F32),
               pltpu.VMEM((tm, d), BF16),
               pltpu.VMEM((tm, d_ff), BF16)]
    body = functools.partial(_ffn_kernel, tm=tm, d_ff=d_ff, final_norm=final_norm)
    return _call(body, x, consts, scratch, tm, "convffn")


def kernel(x, mix_norm, ffn_norm, ab_w_in, pool_w, pool_b, pool_scale, sc_conv_w, sc_conv_b,
           ab_w_out, gla_w_in, gla_w_g2, gla_b_g, gla_norm, gla_w_out, ffn_w_up, ffn_conv_w,
           ffn_conv_b, ffn_w_down, final_norm):
    depth = mix_norm.shape[0]
    tm = 512
    assert x.shape[1] % tm == 0 and tm % CHUNK == 0
    for l in range(depth):
        i = l // 2
        if l % 2 == 0:
            x = _ab_layer(x, mix_norm[l], ab_w_in[i], pool_w[i], pool_b[i], pool_scale[i],
                          sc_conv_w[i], sc_conv_b[i], ab_w_out[i], tm)
        else:
            x = _gla_layer(x, mix_norm[l], gla_w_in[i], gla_w_g2[i], gla_b_g[i], gla_norm[i],
                           gla_w_out[i], tm)
        x = _ffn_layer(x, ffn_norm[l], ffn_w_up[l], ffn_conv_w[l], ffn_conv_b[l], ffn_w_down[l],
                       final_norm, tm, final_norm=(l == depth - 1))
    return x
```

```python
import functools
import math

import jax
import jax.numpy as jnp
from jax import lax
from jax.experimental import pallas as pl
from jax.experimental.pallas import tpu as pltpu

F32 = jnp.float32
BF16 = jnp.bfloat16

EPS = 1e-6
CHUNK = 64
POOL_WINDOWS = (2, 4, 8, 16)
POOL_GROUP = 128
CONV_WIDTH = 3
GLA_HEADS = 4
GLA_RANK = 16
GLA_TAU = 16.0

LANES = 128
SUBLANES = 8
MXU_COLS = 256
POOL_HIST = 16
CONV_HIST = SUBLANES
FF_CHUNK = 2 * MXU_COLS
VMEM_LIMIT_BYTES = 56 * 1024 * 1024

SQRT_HALF = math.sqrt(0.5)


def _rms(x, g):
    ms = jnp.mean(x * x, axis=-1, keepdims=True)
    return x * lax.rsqrt(ms + EPS) * g


def _shifted(hist, cur, shift):
    ext = jnp.concatenate([hist, cur], axis=0)
    return pltpu.roll(ext, shift, axis=0)[hist.shape[0]:]


def _causal_conv3(hist, u, w, b):
    u1 = _shifted(hist, u, 1)
    u2 = _shifted(hist, u, 2)
    return b + w[0:1] * u2 + w[1:2] * u1 + w[2:3] * u


def _dot(a, b):
    return jnp.dot(a, b, preferred_element_type=F32)


def _ab_kernel(x_ref, g_ref, w_in_ref, pw_ref, pb_ref, ps_ref, cw_ref, cb_ref, w_out_ref,
               o_ref, pool_hist, conv_hist, *, tm, pool_width, sc_width):
    s = pl.program_id(1)

    @pl.when(s == 0)
    def _():
        pool_hist[...] = jnp.zeros_like(pool_hist)
        conv_hist[...] = jnp.zeros_like(conv_hist)

    x = x_ref[...]
    hn = _rms(x, g_ref[...]).astype(BF16)
    h = _dot(hn, w_in_ref[...])
    pu = h[:, :pool_width]
    sb = h[:, pool_width:pool_width + sc_width]
    sc = h[:, pool_width + sc_width:pool_width + 2 * sc_width]
    sx = h[:, pool_width + 2 * sc_width:]

    frame = lax.broadcasted_iota(jnp.int32, (tm, POOL_GROUP), 0) + (s * tm + 1)
    ext = jnp.concatenate([pool_hist[...], pu], axis=0)
    ya = []
    for gi, win in enumerate(POOL_WINDOWS):
        lo = gi * POOL_GROUP
        e = ext[:, lo:lo + POOL_GROUP]
        sh = 1
        while sh < win:
            e = e + pltpu.roll(e, sh, axis=0)
            sh *= 2
        ug = pu[:, lo:lo + POOL_GROUP]
        mean = e[POOL_HIST:] / jnp.minimum(frame, win).astype(F32)
        ya.append(_dot((mean - ug).astype(BF16), pw_ref[gi]))
    ya = (jnp.concatenate(ya, axis=-1) + pb_ref[...]) * ps_ref[...]

    z = sc * sx
    yb = sb * _causal_conv3(conv_hist[...], z, cw_ref[...], cb_ref[...])

    y = jnp.concatenate([ya, yb], axis=-1).astype(BF16)
    o_ref[...] = x + _dot(y, w_out_ref[...])

    pool_hist[...] = pu[tm - POOL_HIST:]
    conv_hist[...] = z[tm - CONV_HIST:]


def _chunk_cumsum(e, row):
    sh = 1
    while sh < SUBLANES:
        e = e + jnp.where(row >= sh, pltpu.roll(e, sh, axis=0), 0.0)
        sh *= 2
    while sh < CHUNK:
        e = e + jnp.concatenate([jnp.zeros((sh, e.shape[1]), F32), e[:CHUNK - sh]], axis=0)
        sh *= 2
    return e


def _gla_kernel(x_ref, g_ref, wq_ref, wk_ref, wv_ref, wr_ref, wgl_ref, wg2_ref, bg_ref, ng_ref,
                w_out_ref, o_ref, state_ref, hn_s, q_s, k_s, v_s, gate_s, rg_s, y_s, *, tm, hk, hv):
    s = pl.program_id(1)

    @pl.when(s == 0)
    def _():
        state_ref[...] = jnp.zeros_like(state_ref)

    hn_s[...] = _rms(x_ref[...], g_ref[...]).astype(BF16)
    gl = _dot(hn_s[...], wgl_ref[...])
    gpre = _dot(gl.astype(BF16), wg2_ref[...]) + bg_ref[...]
    q_s[...] = _dot(hn_s[...], wq_ref[...]) * (hk ** -0.5)
    k_s[...] = _dot(hn_s[...], wk_ref[...])
    gate_s[...] = (jnp.minimum(gpre, 0.0) - jnp.log(1.0 + jnp.exp(-jnp.abs(gpre)))) * (1.0 / GLA_TAU)
    v_s[...] = _dot(hn_s[...], wv_ref[...]).astype(BF16)
    r = _dot(hn_s[...], wr_ref[...])
    rg_s[...] = r * jax.nn.sigmoid(r)

    ri = lax.broadcasted_iota(jnp.int32, (CHUNK, CHUNK), 0)
    ci = lax.broadcasted_iota(jnp.int32, (CHUNK, CHUNK), 1)
    causal = ri >= ci
    row = lax.broadcasted_iota(jnp.int32, (CHUNK, gate_s.shape[1]), 0)
    ng = ng_ref[...]

    n_chunks = tm // CHUNK
    nt = (((1,), (1,)), ((), ()))
    tn = (((0,), (0,)), ((), ()))
    heads = [(slice(hd * hk, (hd + 1) * hk), slice(hd * hv, (hd + 1) * hv))
             for hd in range(GLA_HEADS)]

    def chunk_prep(c):
        rows = slice(c * CHUNK, (c + 1) * CHUNK)
        bcum = _chunk_cumsum(gate_s[rows, :], row)
        blast = bcum[CHUNK - 1:CHUNK]
        kc = k_s[rows, :]
        q_in = (q_s[rows, :] * jnp.exp(bcum)).astype(BF16)
        k_in = (kc * jnp.exp(-bcum)).astype(BF16)
        k_out = (kc * jnp.exp(blast - bcum)).astype(BF16)
        decay = jnp.exp(blast)
        atts = []
        for ks, _ in heads:
            att = lax.dot_general(q_in[:, ks], k_in[:, ks], nt, preferred_element_type=F32)
            atts.append(jnp.where(causal, att, 0.0).astype(BF16))
        return rows, q_in, k_out, decay, atts

    states = [state_ref[hd] for hd in range(GLA_HEADS)]
    prep = chunk_prep(0)
    for c in range(n_chunks):
        nxt = chunk_prep(c + 1) if c + 1 < n_chunks else None
        rows, q_in, k_out, decay, atts = prep
        inter = [lax.dot_general(q_in[:, ks], states[hd].astype(BF16), nt,
                                 preferred_element_type=F32)
                 for hd, (ks, _) in enumerate(heads)]
        kv_t = [lax.dot_general(v_s[rows, vs], k_out[:, ks], tn, preferred_element_type=F32)
                for ks, vs in heads]
        for hd, (ks, vs) in enumerate(heads):
            o = _dot(atts[hd], v_s[rows, vs]) + inter[hd]
            states[hd] = states[hd] * decay[:, ks] + kv_t[hd]
            o = o * lax.rsqrt(jnp.mean(o * o, axis=-1, keepdims=True) + EPS) * ng
            y_s[rows, vs] = (o * rg_s[rows, vs]).astype(BF16)
        if (c + 1) % (n_chunks // 2) == 0:
            half = slice((c + 1 - n_chunks // 2) * CHUNK, (c + 1) * CHUNK)
            o_ref[half, :] = x_ref[half, :] + _dot(y_s[half, :], w_out_ref[...])
        prep = nxt

    for hd in range(GLA_HEADS):
        state_ref[hd] = states[hd]


def _ffn_kernel(x_ref, g_ref, wup_ref, cw_ref, cb_ref, wdn_ref, fg_ref, o_ref, conv_hist, hn_s, a_s,
                *, tm, d_ff, final_norm):
    s = pl.program_id(1)

    @pl.when(s == 0)
    def _():
        conv_hist[...] = jnp.zeros_like(conv_hist)

    x = x_ref[...]
    hn_s[...] = _rms(x, g_ref[...]).astype(BF16)

    for c0 in range(0, d_ff, FF_CHUNK):
        cols = slice(c0, min(c0 + FF_CHUNK, d_ff))
        gcols = slice(d_ff + cols.start, d_ff + cols.stop)
        u = _dot(hn_s[...], wup_ref[:, cols])
        v = _dot(hn_s[...], wup_ref[:, gcols])
        uc = _causal_conv3(conv_hist[:, cols], u, cw_ref[:, cols], cb_ref[:, cols])
        a_s[:, cols] = (0.5 * uc * (1.0 + lax.erf(uc * SQRT_HALF)) * v).astype(BF16)
        conv_hist[:, cols] = u[tm - CONV_HIST:]

    out = x + _dot(a_s[...], wdn_ref[...])
    if final_norm:
        out = _rms(out, fg_ref[...])
    o_ref[...] = out


def _const_spec(shape):
    nd = len(shape)
    return pl.BlockSpec(shape, lambda b, s: (0,) * nd, pipeline_mode=pl.Buffered(1))


def _tile_spec(tm, d):
    return pl.BlockSpec((None, tm, d), lambda b, s: (b, s, 0))


def _call(body, x, consts, scratch, tm, name):
    bsz, seq, d = x.shape
    return pl.pallas_call(
        body,
        name=name,
        grid=(bsz, seq // tm),
        in_specs=[_tile_spec(tm, d)] + [_const_spec(c.shape) for c in consts],
        out_specs=_tile_spec(tm, d),
        out_shape=jax.ShapeDtypeStruct(x.shape, x.dtype),
        scratch_shapes=scratch,
        compiler_params=pltpu.CompilerParams(
            dimension_semantics=("arbitrary", "arbitrary"),
            vmem_limit_bytes=VMEM_LIMIT_BYTES),
    )(x, *consts)


def _row(v):
    return v.reshape(1, -1).astype(F32)


def _ab_layer(x, norm_g, w_in, pool_w, pool_b, pool_scale, conv_w, conv_b, w_out, tm):
    pool_width = pool_b.shape[0]
    sc_width = conv_b.shape[0]
    consts = [_row(norm_g), w_in.astype(BF16), pool_w.astype(BF16), _row(pool_b), _row(pool_scale),
              conv_w.astype(F32), _row(conv_b), w_out.astype(BF16)]
    scratch = [pltpu.VMEM((POOL_HIST, pool_width), F32), pltpu.VMEM((CONV_HIST, sc_width), F32)]
    body = functools.partial(_ab_kernel, tm=tm, pool_width=pool_width, sc_width=sc_width)
    return _call(body, x, consts, scratch, tm, "pool_conv_mixer")


def _gla_layer(x, norm_g, w_in, w_g2, b_g, head_norm, w_out, tm):
    d = x.shape[-1]
    dk = w_g2.shape[1]
    dv = w_out.shape[0]
    hk = dk // GLA_HEADS
    hv = dv // GLA_HEADS
    wq = w_in[:, :dk]
    wk = w_in[:, dk:2 * dk]
    wv = w_in[:, 2 * dk:2 * dk + dv]
    wr = w_in[:, 2 * dk + dv:2 * dk + 2 * dv]
    wgl = jnp.pad(w_in[:, 2 * dk + 2 * dv:], ((0, 0), (0, LANES - GLA_RANK)))
    wg2 = jnp.pad(w_g2, ((0, LANES - GLA_RANK), (0, 0)))
    consts = [_row(norm_g), wq.astype(BF16), wk.astype(BF16), wv.astype(BF16), wr.astype(BF16),
              wgl.astype(BF16), wg2.astype(BF16), _row(b_g), _row(head_norm), w_out.astype(BF16)]
    scratch = [pltpu.VMEM((GLA_HEADS, hv, hk), F32),
               pltpu.VMEM((tm, d), BF16),
               pltpu.VMEM((tm, dk), F32),
               pltpu.VMEM((tm, dk), F32),
               pltpu.VMEM((tm, dv), BF16),
               pltpu.VMEM((tm, dk), F32),
               pltpu.VMEM((tm, dv), F32),
               pltpu.VMEM((tm, dv), BF16)]
    body = functools.partial(_gla_kernel, tm=tm, hk=hk, hv=hv)
    return _call(body, x, consts, scratch, tm, "gla_mixer")


def _ffn_layer(x, norm_g, w_up, conv_w, conv_b, w_down, final_g, tm, final_norm):
    d = x.shape[-1]
    d_ff = w_down.shape[0]
    assert d_ff % LANES == 0
    consts = [_row(norm_g), w_up.astype(BF16), conv_w.astype(F32), _row(conv_b),
              w_down.astype(BF16), _row(final_g)]
    scratch = [pltpu.VMEM((CONV_HIST, d_ff), F32),
               pltpu.VMEM((tm, d), BF16),
               pltpu.VMEM((tm, d_ff), BF16)]
    body = functools.partial(_ffn_kernel, tm=tm, d_ff=d_ff, final_norm=final_norm)
    return _call(body, x, consts, scratch, tm, "convffn")


def kernel(x, mix_norm, ffn_norm, ab_w_in, pool_w, pool_b, pool_scale, sc_conv_w, sc_conv_b,
           ab_w_out, gla_w_in, gla_w_g2, gla_b_g, gla_norm, gla_w_out, ffn_w_up, ffn_conv_w,
           ffn_conv_b, ffn_w_down, final_norm):
    depth = mix_norm.shape[0]
    tm = 512
    assert x.shape[1] % tm == 0 and tm % CHUNK == 0
    for l in range(depth):
        i = l // 2
        if l % 2 == 0:
            x = _ab_layer(x, mix_norm[l], ab_w_in[i], pool_w[i], pool_b[i], pool_scale[i],
                          sc_conv_w[i], sc_conv_b[i], ab_w_out[i], tm)
        else:
            x = _gla_layer(x, mix_norm[l], gla_w_in[i], gla_w_g2[i], gla_b_g[i], gla_norm[i],
                           gla_w_out[i], tm)
        x = _ffn_layer(x, ffn_norm[l], ffn_w_up[l], ffn_conv_w[l], ffn_conv_b[l], ffn_w_down[l],
                       final_norm, tm, final_norm=(l == depth - 1))
    return x
```

```python
import functools
import math

import jax
import jax.numpy as jnp
from jax import lax
from jax.experimental import pallas as pl
from jax.experimental.pallas import tpu as pltpu

F32 = jnp.float32
BF16 = jnp.bfloat16

EPS = 1e-6
CHUNK = 64
POOL_WINDOWS = (2, 4, 8, 16)
POOL_GROUP = 128
CONV_WIDTH = 3
GLA_HEADS = 4
GLA_RANK = 16
GLA_TAU = 16.0

LANES = 128
SUBLANES = 8
MXU_COLS = 256
POOL_HIST = 16
CONV_HIST = SUBLANES
FF_CHUNK = 2 * MXU_COLS
AB_TILE = 1024
GLA_TILE = 1024
FFN_TILE = 1024
VMEM_LIMIT_BYTES = 56 * 1024 * 1024

SQRT_HALF = math.sqrt(0.5)


def _rms(x, g):
    ms = jnp.mean(x * x, axis=-1, keepdims=True)
    return x * lax.rsqrt(ms + EPS) * g


def _shifted(hist, cur, shift):
    ext = jnp.concatenate([hist, cur], axis=0)
    return pltpu.roll(ext, shift, axis=0)[hist.shape[0]:]


def _causal_conv3(hist, u, w, b):
    u1 = _shifted(hist, u, 1)
    u2 = _shifted(hist, u, 2)
    return b + w[0:1] * u2 + w[1:2] * u1 + w[2:3] * u


def _dot(a, b):
    return jnp.dot(a, b, preferred_element_type=F32)


def _ab_kernel(x_ref, g_ref, w_in_ref, pw_ref, pb_ref, ps_ref, cw_ref, cb_ref, w_out_ref,
               o_ref, pool_hist, conv_hist, *, tm, pool_width, sc_width):
    s = pl.program_id(1)

    @pl.when(s == 0)
    def _():
        pool_hist[...] = jnp.zeros_like(pool_hist)
        conv_hist[...] = jnp.zeros_like(conv_hist)

    x = x_ref[...]
    hn = _rms(x, g_ref[...]).astype(BF16)
    h = _dot(hn, w_in_ref[...])
    pu = h[:, :pool_width]
    sb = h[:, pool_width:pool_width + sc_width]
    sc = h[:, pool_width + sc_width:pool_width + 2 * sc_width]
    sx = h[:, pool_width + 2 * sc_width:]

    frame = lax.broadcasted_iota(jnp.int32, (tm, POOL_GROUP), 0) + (s * tm + 1)
    ext = jnp.concatenate([pool_hist[...], pu], axis=0)
    ya = []
    for gi, win in enumerate(POOL_WINDOWS):
        lo = gi * POOL_GROUP
        e = ext[:, lo:lo + POOL_GROUP]
        sh = 1
        while sh < win:
            e = e + pltpu.roll(e, sh, axis=0)
            sh *= 2
        ug = pu[:, lo:lo + POOL_GROUP]
        mean = e[POOL_HIST:] / jnp.minimum(frame, win).astype(F32)
        ya.append(_dot((mean - ug).astype(BF16), pw_ref[gi]))
    ya = (jnp.concatenate(ya, axis=-1) + pb_ref[...]) * ps_ref[...]

    z = sc * sx
    yb = sb * _causal_conv3(conv_hist[...], z, cw_ref[...], cb_ref[...])

    y = jnp.concatenate([ya, yb], axis=-1).astype(BF16)
    o_ref[...] = x + _dot(y, w_out_ref[...])

    pool_hist[...] = pu[tm - POOL_HIST:]
    conv_hist[...] = z[tm - CONV_HIST:]


def _chunk_cumsum(e, row):
    sh = 1
    while sh < SUBLANES:
        e = e + jnp.where(row >= sh, pltpu.roll(e, sh, axis=0), 0.0)
        sh *= 2
    while sh < CHUNK:
        e = e + jnp.concatenate([jnp.zeros((sh, e.shape[1]), F32), e[:CHUNK - sh]], axis=0)
        sh *= 2
    return e


def _gla_kernel(x_ref, g_ref, wq_ref, wk_ref, wv_ref, wr_ref, wgl_ref, wg2_ref, bg_ref, ng_ref,
                w_out_ref, o_ref, state_ref, hn_s, q_s, k_s, v_s, gate_s, rg_s, y_s, *, tm, hk, hv):
    s = pl.program_id(1)

    @pl.when(s == 0)
    def _():
        state_ref[...] = jnp.zeros_like(state_ref)

    hn_s[...] = _rms(x_ref[...], g_ref[...]).astype(BF16)
    gl = _dot(hn_s[...], wgl_ref[...])
    gpre = _dot(gl.astype(BF16), wg2_ref[...]) + bg_ref[...]
    q_s[...] = _dot(hn_s[...], wq_ref[...]) * (hk ** -0.5)
    k_s[...] = _dot(hn_s[...], wk_ref[...])
    gate_s[...] = (jnp.minimum(gpre, 0.0) - jnp.log(1.0 + jnp.exp(-jnp.abs(gpre)))) * (1.0 / GLA_TAU)
    v_s[...] = _dot(hn_s[...], wv_ref[...]).astype(BF16)
    r = _dot(hn_s[...], wr_ref[...])
    rg_s[...] = r * jax.nn.sigmoid(r)

    ri = lax.broadcasted_iota(jnp.int32, (CHUNK, CHUNK), 0)
    ci = lax.broadcasted_iota(jnp.int32, (CHUNK, CHUNK), 1)
    causal = ri >= ci
    row = lax.broadcasted_iota(jnp.int32, (CHUNK, gate_s.shape[1]), 0)
    ng = ng_ref[...]

    n_chunks = tm // CHUNK
    nt = (((1,), (1,)), ((), ()))
    tn = (((0,), (0,)), ((), ()))
    heads = [(slice(hd * hk, (hd + 1) * hk), slice(hd * hv, (hd + 1) * hv))
             for hd in range(GLA_HEADS)]

    def chunk_prep(c):
        rows = slice(c * CHUNK, (c + 1) * CHUNK)
        bcum = _chunk_cumsum(gate_s[rows, :], row)
        blast = bcum[CHUNK - 1:CHUNK]
        kc = k_s[rows, :]
        q_in = (q_s[rows, :] * jnp.exp(bcum)).astype(BF16)
        k_in = (kc * jnp.exp(-bcum)).astype(BF16)
        k_out = (kc * jnp.exp(blast - bcum)).astype(BF16)
        decay = jnp.exp(blast)
        atts, dcols = [], []
        for ks, _ in heads:
            att = lax.dot_general(q_in[:, ks], k_in[:, ks], nt, preferred_element_type=F32)
            atts.append(jnp.where(causal, att, 0.0).astype(BF16))
            dcols.append(jnp.broadcast_to(decay[:, ks], (hk, hk)).T)
        return rows, q_in, k_out, dcols, atts

    states = [state_ref[hd] for hd in range(GLA_HEADS)]
    prep = chunk_prep(0)
    for c in range(n_chunks):
        nxt = chunk_prep(c + 1) if c + 1 < n_chunks else None
        rows, q_in, k_out, dcols, atts = prep
        kv = [lax.dot_general(k_out[:, ks], v_s[rows, vs], tn, preferred_element_type=F32)
              for ks, vs in heads]
        for hd, (ks, vs) in enumerate(heads):
            lhs = jnp.concatenate([q_in[:, ks], atts[hd]], axis=1)
            rhs = jnp.concatenate([states[hd].astype(BF16), v_s[rows, vs]], axis=0)
            o = _dot(lhs, rhs)
            states[hd] = states[hd] * jnp.concatenate([dcols[hd]] * (hv // hk), axis=1) + kv[hd]
            o = o * lax.rsqrt(jnp.mean(o * o, axis=-1, keepdims=True) + EPS) * ng
            y_s[rows, vs] = (o * rg_s[rows, vs]).astype(BF16)
        if (c + 1) % (n_chunks // 2) == 0:
            half = slice((c + 1 - n_chunks // 2) * CHUNK, (c + 1) * CHUNK)
            o_ref[half, :] = x_ref[half, :] + _dot(y_s[half, :], w_out_ref[...])
        prep = nxt

    for hd in range(GLA_HEADS):
        state_ref[hd] = states[hd]


def _ffn_kernel(x_ref, g_ref, wup_ref, cw_ref, cb_ref, wdn_ref, fg_ref, o_ref, conv_hist, hn_s, a_s,
                *, tm, d_ff, final_norm):
    s = pl.program_id(1)

    @pl.when(s == 0)
    def _():
        conv_hist[...] = jnp.zeros_like(conv_hist)

    x = x_ref[...]
    hn_s[...] = _rms(x, g_ref[...]).astype(BF16)

    for c0 in range(0, d_ff, FF_CHUNK):
        cols = slice(c0, min(c0 + FF_CHUNK, d_ff))
        gcols = slice(d_ff + cols.start, d_ff + cols.stop)
        u = _dot(hn_s[...], wup_ref[:, cols])
        v = _dot(hn_s[...], wup_ref[:, gcols])
        uc = _causal_conv3(conv_hist[:, cols], u, cw_ref[:, cols], cb_ref[:, cols])
        a_s[:, cols] = (0.5 * uc * (1.0 + lax.erf(uc * SQRT_HALF)) * v).astype(BF16)
        conv_hist[:, cols] = u[tm - CONV_HIST:]

    out = x + _dot(a_s[...], wdn_ref[...])
    if final_norm:
        out = _rms(out, fg_ref[...])
    o_ref[...] = out


def _const_spec(shape):
    nd = len(shape)
    return pl.BlockSpec(shape, lambda b, s: (0,) * nd, pipeline_mode=pl.Buffered(1))


def _tile_spec(tm, d):
    return pl.BlockSpec((None, tm, d), lambda b, s: (b, s, 0))


def _call(body, x, consts, scratch, tm, name):
    bsz, seq, d = x.shape
    return pl.pallas_call(
        body,
        name=name,
        grid=(bsz, seq // tm),
        in_specs=[_tile_spec(tm, d)] + [_const_spec(c.shape) for c in consts],
        out_specs=_tile_spec(tm, d),
        out_shape=jax.ShapeDtypeStruct(x.shape, x.dtype),
        scratch_shapes=scratch,
        compiler_params=pltpu.CompilerParams(
            dimension_semantics=("arbitrary", "arbitrary"),
            vmem_limit_bytes=VMEM_LIMIT_BYTES),
    )(x, *consts)


def _row(v):
    return v.reshape(1, -1).astype(F32)


def _ab_layer(x, norm_g, w_in, pool_w, pool_b, pool_scale, conv_w, conv_b, w_out, tm):
    pool_width = pool_b.shape[0]
    sc_width = conv_b.shape[0]
    consts = [_row(norm_g), w_in.astype(BF16), pool_w.astype(BF16), _row(pool_b), _row(pool_scale),
              conv_w.astype(F32), _row(conv_b), w_out.astype(BF16)]
    scratch = [pltpu.VMEM((POOL_HIST, pool_width), F32), pltpu.VMEM((CONV_HIST, sc_width), F32)]
    body = functools.partial(_ab_kernel, tm=tm, pool_width=pool_width, sc_width=sc_width)
    return _call(body, x, consts, scratch, tm, "pool_conv_mixer")


def _gla_layer(x, norm_g, w_in, w_g2, b_g, head_norm, w_out, tm):
    d = x.shape[-1]
    dk = w_g2.shape[1]
    dv = w_out.shape[0]
    hk = dk // GLA_HEADS
    hv = dv // GLA_HEADS
    assert hv % hk == 0
    wq = w_in[:, :dk]
    wk = w_in[:, dk:2 * dk]
    wv = w_in[:, 2 * dk:2 * dk + dv]
    wr = w_in[:, 2 * dk + dv:2 * dk + 2 * dv]
    wgl = jnp.pad(w_in[:, 2 * dk + 2 * dv:], ((0, 0), (0, LANES - GLA_RANK)))
    wg2 = jnp.pad(w_g2, ((0, LANES - GLA_RANK), (0, 0)))
    consts = [_row(norm_g), wq.astype(BF16), wk.astype(BF16), wv.astype(BF16), wr.astype(BF16),
              wgl.astype(BF16), wg2.astype(BF16), _row(b_g), _row(head_norm), w_out.astype(BF16)]
    scratch = [pltpu.VMEM((GLA_HEADS, hk, hv), F32),
               pltpu.VMEM((tm, d), BF16),
               pltpu.VMEM((tm, dk), F32),
               pltpu.VMEM((tm, dk), F32),
               pltpu.VMEM((tm, dv), BF16),
               pltpu.VMEM((tm, dk), F32),
               pltpu.VMEM((tm, dv), F32),
               pltpu.VMEM((tm, dv), BF16)]
    body = functools.partial(_gla_kernel, tm=tm, hk=hk, hv=hv)
    return _call(body, x, consts, scratch, tm, "gla_mixer")


def _ffn_layer(x, norm_g, w_up, conv_w, conv_b, w_down, final_g, tm, final_norm):
    d = x.shape[-1]
    d_ff = w_down.shape[0]
    assert d_ff % LANES == 0
    consts = [_row(norm_g), w_up.astype(BF16), conv_w.astype(F32), _row(conv_b),
              w_down.astype(BF16), _row(final_g)]
    scratch = [pltpu.VMEM((CONV_HIST, d_ff), F32),
               pltpu.VMEM((tm, d), BF16),
               pltpu.VMEM((tm, d_ff), BF16)]
    body = functools.partial(_ffn_kernel, tm=tm, d_ff=d_ff, final_norm=final_norm)
    return _call(body, x, consts, scratch, tm, "convffn")


def kernel(x, mix_norm, ffn_norm, ab_w_in, pool_w, pool_b, pool_scale, sc_conv_w, sc_conv_b,
           ab_w_out, gla_w_in, gla_w_g2, gla_b_g, gla_norm, gla_w_out, ffn_w_up, ffn_conv_w,
           ffn_conv_b, ffn_w_down, final_norm):
    depth = mix_norm.shape[0]
    seq = x.shape[1]
    assert seq % AB_TILE == 0 and seq % GLA_TILE == 0 and seq % FFN_TILE == 0 and GLA_TILE % CHUNK == 0
    for l in range(depth):
        i = l // 2
        if l % 2 == 0:
            x = _ab_layer(x, mix_norm[l], ab_w_in[i], pool_w[i], pool_b[i], pool_scale[i],
                          sc_conv_w[i], sc_conv_b[i], ab_w_out[i], AB_TILE)
        else:
            x = _gla_layer(x, mix_norm[l], gla_w_in[i], gla_w_g2[i], gla_b_g[i], gla_norm[i],
                           gla_w_out[i], GLA_TILE)
        x = _ffn_layer(x, ffn_norm[l], ffn_w_up[l], ffn_conv_w[l], ffn_conv_b[l], ffn_w_down[l],
                       final_norm, FFN_TILE, final_norm=(l == depth - 1))
    return x
```

```python
import functools
import math

import jax
import jax.numpy as jnp
from jax import lax
from jax.experimental import pallas as pl
from jax.experimental.pallas import tpu as pltpu

F32 = jnp.float32
BF16 = jnp.bfloat16

EPS = 1e-6
CHUNK = 64
POOL_WINDOWS = (2, 4, 8, 16)
POOL_GROUP = 128
CONV_WIDTH = 3
GLA_HEADS = 4
GLA_RANK = 16
GLA_TAU = 16.0

LANES = 128
SUBLANES = 8
MXU_COLS = 256
POOL_HIST = 16
CONV_HIST = SUBLANES
FF_CHUNK = 2 * MXU_COLS
W_COLS = 2 * MXU_COLS
W_ROWS = MXU_COLS
AB_TILE = 1024
GLA_TILE = 1024
FFN_TILE = 1024
VMEM_LIMIT_BYTES = 56 * 1024 * 1024

SQRT_HALF = math.sqrt(0.5)


def _rms(x, g):
    ms = jnp.mean(x * x, axis=-1, keepdims=True)
    return x * lax.rsqrt(ms + EPS) * g


def _shifted(hist, cur, shift):
    ext = jnp.concatenate([hist, cur], axis=0)
    return pltpu.roll(ext, shift, axis=0)[hist.shape[0]:]


def _causal_conv3(hist, u, w, b):
    u1 = _shifted(hist, u, 1)
    u2 = _shifted(hist, u, 2)
    return b + w[0:1] * u2 + w[1:2] * u1 + w[2:3] * u


def _dot(a, b):
    return jnp.dot(a, b, preferred_element_type=F32)


def _load_weights(t, win_ref, win_s, wout_ref, wout_s):
    @pl.when(t < win_s.shape[0])
    def _():
        win_s[t] = win_ref[...].astype(BF16)

    rows = wout_ref.shape[0]

    @pl.when(t < wout_s.shape[0] // rows)
    def _():
        wout_s[pl.ds(pl.multiple_of(t * rows, rows), rows), :] = wout_ref[...].astype(BF16)


def _ab_kernel(x_ref, g_ref, win_ref, pw_ref, pb_ref, ps_ref, cw_ref, cb_ref, wout_ref,
               o_ref, win_s, wout_s, pool_hist, conv_hist, *, tm, n_load, tiles_per_seq):
    t = pl.program_id(0)

    @pl.when(t < n_load)
    def _():
        _load_weights(t, win_ref, win_s, wout_ref, wout_s)

    @pl.when(t >= n_load)
    def _():
        s = lax.rem(t - n_load, tiles_per_seq)

        @pl.when(s == 0)
        def _():
            pool_hist[...] = jnp.zeros_like(pool_hist)
            conv_hist[...] = jnp.zeros_like(conv_hist)

        x = x_ref[...]
        hn = _rms(x, g_ref[...]).astype(BF16)
        pu, sb, sc, sx = (_dot(hn, win_s[j]) for j in range(4))

        frame = lax.broadcasted_iota(jnp.int32, (tm, POOL_GROUP), 0) + (s * tm + 1)
        ext = jnp.concatenate([pool_hist[...], pu], axis=0)
        ya = []
        for gi, win in enumerate(POOL_WINDOWS):
            lo = gi * POOL_GROUP
            e = ext[:, lo:lo + POOL_GROUP]
            sh = 1
            while sh < win:
                e = e + pltpu.roll(e, sh, axis=0)
                sh *= 2
            ug = pu[:, lo:lo + POOL_GROUP]
            mean = e[POOL_HIST:] / jnp.minimum(frame, win).astype(F32)
            ya.append(_dot((mean - ug).astype(BF16), pw_ref[gi]))
        ya = (jnp.concatenate(ya, axis=-1) + pb_ref[...]) * ps_ref[...]

        z = sc * sx
        yb = sb * _causal_conv3(conv_hist[...], z, cw_ref[...], cb_ref[...])

        y = jnp.concatenate([ya, yb], axis=-1).astype(BF16)
        o_ref[...] = x + _dot(y, wout_s[...])

        pool_hist[...] = pu[tm - POOL_HIST:]
        conv_hist[...] = z[tm - CONV_HIST:]


def _chunk_cumsum(e, row):
    sh = 1
    while sh < SUBLANES:
        e = e + jnp.where(row >= sh, pltpu.roll(e, sh, axis=0), 0.0)
        sh *= 2
    while sh < CHUNK:
        e = e + jnp.concatenate([jnp.zeros((sh, e.shape[1]), F32), e[:CHUNK - sh]], axis=0)
        sh *= 2
    return e


def _gla_kernel(x_ref, g_ref, win_ref, wgl_ref, wg2_ref, bg_ref, ng_ref, wout_ref,
                o_ref, win_s, wout_s, state_ref, hn_s, q_s, k_s, v_s, gate_s, rg_s, y_s,
                *, tm, hk, hv, n_load, tiles_per_seq):
    t = pl.program_id(0)

    @pl.when(t < n_load)
    def _():
        _load_weights(t, win_ref, win_s, wout_ref, wout_s)

    @pl.when(t >= n_load)
    def _():
        @pl.when(lax.rem(t - n_load, tiles_per_seq) == 0)
        def _():
            state_ref[...] = jnp.zeros_like(state_ref)

        hn_s[...] = _rms(x_ref[...], g_ref[...]).astype(BF16)
        gl = _dot(hn_s[...], wgl_ref[...])
        gpre = _dot(gl.astype(BF16), wg2_ref[...]) + bg_ref[...]
        q_s[...] = _dot(hn_s[...], win_s[0]) * (hk ** -0.5)
        k_s[...] = _dot(hn_s[...], win_s[1])
        gate_s[...] = (jnp.minimum(gpre, 0.0) - jnp.log(1.0 + jnp.exp(-jnp.abs(gpre)))) * (1.0 / GLA_TAU)
        for j in range(2):
            cols = slice(j * W_COLS, (j + 1) * W_COLS)
            v_s[:, cols] = _dot(hn_s[...], win_s[2 + j]).astype(BF16)
        for j in range(2):
            cols = slice(j * W_COLS, (j + 1) * W_COLS)
            r = _dot(hn_s[...], win_s[4 + j])
            rg_s[:, cols] = r * jax.nn.sigmoid(r)

        ri = lax.broadcasted_iota(jnp.int32, (CHUNK, CHUNK), 0)
        ci = lax.broadcasted_iota(jnp.int32, (CHUNK, CHUNK), 1)
        causal = ri >= ci
        row = lax.broadcasted_iota(jnp.int32, (CHUNK, gate_s.shape[1]), 0)
        ng = ng_ref[...]

        n_chunks = tm // CHUNK
        nt = (((1,), (1,)), ((), ()))
        tn = (((0,), (0,)), ((), ()))
        heads = [(slice(hd * hk, (hd + 1) * hk), slice(hd * hv, (hd + 1) * hv))
                 for hd in range(GLA_HEADS)]

        def chunk_prep(c):
            rows = slice(c * CHUNK, (c + 1) * CHUNK)
            bcum = _chunk_cumsum(gate_s[rows, :], row)
            blast = bcum[CHUNK - 1:CHUNK]
            kc = k_s[rows, :]
            q_in = (q_s[rows, :] * jnp.exp(bcum)).astype(BF16)
            k_in = (kc * jnp.exp(-bcum)).astype(BF16)
            k_out = (kc * jnp.exp(blast - bcum)).astype(BF16)
            decay = jnp.exp(blast)
            atts, dcols = [], []
            for ks, _ in heads:
                att = lax.dot_general(q_in[:, ks], k_in[:, ks], nt, preferred_element_type=F32)
                atts.append(jnp.where(causal, att, 0.0).astype(BF16))
                dcols.append(jnp.broadcast_to(decay[:, ks], (hk, hk)).T)
            return rows, q_in, k_out, dcols, atts

        states = [state_ref[hd] for hd in range(GLA_HEADS)]
        prep = chunk_prep(0)
        for c in range(n_chunks):
            nxt = chunk_prep(c + 1) if c + 1 < n_chunks else None
            rows, q_in, k_out, dcols, atts = prep
            kv = [lax.dot_general(k_out[:, ks], v_s[rows, vs], tn, preferred_element_type=F32)
                  for ks, vs in heads]
            for hd, (ks, vs) in enumerate(heads):
                lhs = jnp.concatenate([q_in[:, ks], atts[hd]], axis=1)
                rhs = jnp.concatenate([states[hd].astype(BF16), v_s[rows, vs]], axis=0)
                o = _dot(lhs, rhs)
                states[hd] = states[hd] * jnp.concatenate([dcols[hd]] * (hv // hk), axis=1) + kv[hd]
                o = o * lax.rsqrt(jnp.mean(o * o, axis=-1, keepdims=True) + EPS) * ng
                y_s[rows, vs] = (o * rg_s[rows, vs]).astype(BF16)
            if (c + 1) % (n_chunks // 2) == 0:
                half = slice((c + 1 - n_chunks // 2) * CHUNK, (c + 1) * CHUNK)
                o_ref[half, :] = x_ref[half, :] + _dot(y_s[half, :], wout_s[...])
            prep = nxt

        for hd in range(GLA_HEADS):
            state_ref[hd] = states[hd]


def _ffn_kernel(x_ref, g_ref, wup_ref, cw_ref, cb_ref, wdn_ref, fg_ref, o_ref, wup_s, wdn_s,
                conv_hist, hn_s, a_s, *, tm, d_ff, final_norm, n_load, tiles_per_seq):
    t = pl.program_id(0)

    @pl.when(t < n_load)
    def _():
        _load_weights(t, wup_ref, wup_s, wdn_ref, wdn_s)

    @pl.when(t >= n_load)
    def _():
        @pl.when(lax.rem(t - n_load, tiles_per_seq) == 0)
        def _():
            conv_hist[...] = jnp.zeros_like(conv_hist)

        x = x_ref[...]
        hn_s[...] = _rms(x, g_ref[...]).astype(BF16)

        def up_proj(c0, width):
            pieces = []
            while width > 0:
                j, lo = divmod(c0, W_COLS)
                w = min(width, W_COLS - lo)
                pieces.append(_dot(hn_s[...], wup_s[j, :, lo:lo + w]))
                c0 += w
                width -= w
            return pieces[0] if len(pieces) == 1 else jnp.concatenate(pieces, axis=1)

        for c0 in range(0, d_ff, FF_CHUNK):
            cols = slice(c0, min(c0 + FF_CHUNK, d_ff))
            u = up_proj(c0, cols.stop - c0)
            v = up_proj(d_ff + c0, cols.stop - c0)
            uc = _causal_conv3(conv_hist[:, cols], u, cw_ref[:, cols], cb_ref[:, cols])
            a_s[:, cols] = (0.5 * uc * (1.0 + lax.erf(uc * SQRT_HALF)) * v).astype(BF16)
            conv_hist[:, cols] = u[tm - CONV_HIST:]

        out = x + _dot(a_s[...], wdn_s[...])
        if final_norm:
            out = _rms(out, fg_ref[...])
        o_ref[...] = out


def _const_spec(shape):
    nd = len(shape)
    return pl.BlockSpec(shape, lambda t: (0,) * nd, pipeline_mode=pl.Buffered(1))


def _col_chunk_spec(w, layer, lo, n):
    return pl.BlockSpec((None, w.shape[1], W_COLS), lambda t: (layer, 0, lo + jnp.minimum(t, n - 1)))


def _row_chunk_spec(w, layer, n):
    return pl.BlockSpec((None, W_ROWS, w.shape[2]), lambda t: (layer, jnp.minimum(t, n - 1), 0))


def _call(body, x, operands, in_specs, scratch, tm, n_load, name):
    bsz, seq, d = x.shape
    n_tiles = bsz * seq // tm
    tile_spec = pl.BlockSpec((tm, d), lambda t: (jnp.maximum(t - n_load, 0), 0))
    out = pl.pallas_call(
        functools.partial(body, tm=tm, n_load=n_load, tiles_per_seq=seq // tm),
        name=name,
        grid=(n_load + n_tiles,),
        in_specs=[tile_spec] + in_specs,
        out_specs=tile_spec,
        out_shape=jax.ShapeDtypeStruct((bsz * seq, d), x.dtype),
        scratch_shapes=scratch,
        compiler_params=pltpu.CompilerParams(
            dimension_semantics=("arbitrary",),
            vmem_limit_bytes=VMEM_LIMIT_BYTES),
    )(x.reshape(bsz * seq, d), *operands)
    return out.reshape(bsz, seq, d)


def _row(v):
    return v.reshape(1, -1).astype(F32)


def _ab_layer(x, i, norm_g, w_in, pool_w, pool_b, pool_scale, conv_w, conv_b, w_out):
    d = x.shape[-1]
    d_in, d_mid = w_in.shape[2], w_out.shape[1]
    assert pool_b.shape[0] == conv_b.shape[0] == W_COLS and d_in == 4 * W_COLS
    assert d_mid % W_ROWS == 0 and w_out.shape[2] == d
    n_in, n_out = d_in // W_COLS, d_mid // W_ROWS
    n_load = max(n_in, n_out)
    consts = [_row(norm_g), pool_w.astype(BF16), _row(pool_b), _row(pool_scale),
              conv_w.astype(F32), _row(conv_b)]
    operands = [consts[0], w_in] + consts[1:] + [w_out]
    in_specs = ([_const_spec(consts[0].shape), _col_chunk_spec(w_in, i, 0, n_in)]
                + [_const_spec(c.shape) for c in consts[1:]] + [_row_chunk_spec(w_out, i, n_out)])
    scratch = [pltpu.VMEM((n_in, d, W_COLS), BF16),
               pltpu.VMEM((d_mid, d), BF16),
               pltpu.VMEM((POOL_HIST, W_COLS), F32), pltpu.VMEM((CONV_HIST, W_COLS), F32)]
    return _call(_ab_kernel, x, operands, in_specs, scratch, AB_TILE, n_load, "pool_conv_mixer")


def _gla_layer(x, i, norm_g, w_in, w_g2, b_g, head_norm, w_out):
    d = x.shape[-1]
    dk = w_g2.shape[1]
    dv = w_out.shape[1]
    hk = dk // GLA_HEADS
    hv = dv // GLA_HEADS
    assert dk == W_COLS and dv == 2 * W_COLS and hv % hk == 0 and dv % W_ROWS == 0
    n_in, n_out = (2 * dk + 2 * dv) // W_COLS, dv // W_ROWS
    n_load = max(n_in, n_out)
    wgl = jnp.pad(w_in[i, :, 2 * dk + 2 * dv:], ((0, 0), (0, LANES - GLA_RANK)))
    wg2 = jnp.pad(w_g2, ((0, LANES - GLA_RANK), (0, 0)))
    consts = [wgl.astype(BF16), wg2.astype(BF16), _row(b_g), _row(head_norm)]
    operands = [_row(norm_g), w_in] + consts + [w_out]
    in_specs = ([_const_spec((1, d)), _col_chunk_spec(w_in, i, 0, n_in)]
                + [_const_spec(c.shape) for c in consts] + [_row_chunk_spec(w_out, i, n_out)])
    tm = GLA_TILE
    scratch = [pltpu.VMEM((n_in, d, W_COLS), BF16),
               pltpu.VMEM((dv, d), BF16),
               pltpu.VMEM((GLA_HEADS, hk, hv), F32),
               pltpu.VMEM((tm, d), BF16),
               pltpu.VMEM((tm, dk), F32),
               pltpu.VMEM((tm, dk), F32),
               pltpu.VMEM((tm, dv), BF16),
               pltpu.VMEM((tm, dk), F32),
               pltpu.VMEM((tm, dv), F32),
               pltpu.VMEM((tm, dv), BF16)]
    body = functools.partial(_gla_kernel, hk=hk, hv=hv)
    return _call(body, x, operands, in_specs, scratch, tm, n_load, "gla_mixer")


def _ffn_layer(x, l, norm_g, w_up, conv_w, conv_b, w_down, final_g, final_norm):
    d = x.shape[-1]
    d_ff = w_down.shape[1]
    assert d_ff % MXU_COLS == 0 and (2 * d_ff) % W_COLS == 0 and d_ff % W_ROWS == 0
    n_in, n_out = 2 * d_ff // W_COLS, d_ff // W_ROWS
    n_load = max(n_in, n_out)
    consts = [conv_w.astype(F32), _row(conv_b)]
    operands = [_row(norm_g), w_up] + consts + [w_down, _row(final_g)]
    in_specs = ([_const_spec((1, d)), _col_chunk_spec(w_up, l, 0, n_in)]
                + [_const_spec(c.shape) for c in consts]
                + [_row_chunk_spec(w_down, l, n_out), _const_spec((1, d))])
    tm = FFN_TILE
    scratch = [pltpu.VMEM((n_in, d, W_COLS), BF16),
               pltpu.VMEM((d_ff, d), BF16),
               pltpu.VMEM((CONV_HIST, d_ff), F32),
               pltpu.VMEM((tm, d), BF16),
               pltpu.VMEM((tm, d_ff), BF16)]
    body = functools.partial(_ffn_kernel, d_ff=d_ff, final_norm=final_norm)
    return _call(body, x, operands, in_specs, scratch, tm, n_load, "convffn")


def kernel(x, mix_norm, ffn_norm, ab_w_in, pool_w, pool_b, pool_scale, sc_conv_w, sc_conv_b,
           ab_w_out, gla_w_in, gla_w_g2, gla_b_g, gla_norm, gla_w_out, ffn_w_up, ffn_conv_w,
           ffn_conv_b, ffn_w_down, final_norm):
    depth = mix_norm.shape[0]
    seq = x.shape[1]
    assert seq % AB_TILE == 0 and seq % GLA_TILE == 0 and seq % FFN_TILE == 0 and GLA_TILE % CHUNK == 0
    for l in range(depth):
        i = l // 2
        if l % 2 == 0:
            x = _ab_layer(x, i, mix_norm[l], ab_w_in, pool_w[i], pool_b[i], pool_scale[i],
                          sc_conv_w[i], sc_conv_b[i], ab_w_out)
        else:
            x = _gla_layer(x, i, mix_norm[l], gla_w_in, gla_w_g2[i], gla_b_g[i], gla_norm[i],
                           gla_w_out)
        x = _ffn_layer(x, l, ffn_norm[l], ffn_w_up, ffn_conv_w[l], ffn_conv_b[l], ffn_w_down,
                       final_norm, final_norm=(l == depth - 1))
    return x
```

```python
import functools
import math

import jax
import jax.numpy as jnp
from jax import lax
from jax.experimental import pallas as pl
from jax.experimental.pallas import tpu as pltpu

F32 = jnp.float32
BF16 = jnp.bfloat16

EPS = 1e-6
CHUNK = 64
POOL_WINDOWS = (2, 4, 8, 16)
POOL_GROUP = 128
CONV_WIDTH = 3
GLA_HEADS = 4
GLA_RANK = 16
GLA_TAU = 16.0

LANES = 128
SUBLANES = 8
MXU_COLS = 256
POOL_HIST = 16
CONV_HIST = SUBLANES
FF_CHUNK = 2 * MXU_COLS
W_COLS = 2 * MXU_COLS
W_ROWS = MXU_COLS
AB_TILE = 1024
AB_SUB = 512
GLA_TILE = 1024
FFN_TILE = 1024
VMEM_LIMIT_BYTES = 56 * 1024 * 1024

SQRT_HALF = math.sqrt(0.5)


def _rms(x, g):
    ms = jnp.mean(x * x, axis=-1, keepdims=True)
    return x * lax.rsqrt(ms + EPS) * g


def _shifted(hist, cur, shift):
    ext = jnp.concatenate([hist, cur], axis=0)
    return pltpu.roll(ext, shift, axis=0)[hist.shape[0]:]


def _causal_conv3(hist, u, w, b):
    u1 = _shifted(hist, u, 1)
    u2 = _shifted(hist, u, 2)
    return b + w[0:1] * u2 + w[1:2] * u1 + w[2:3] * u


def _dot(a, b):
    return jnp.dot(a, b, preferred_element_type=F32)


def _load_weights(t, win_ref, win_s, wout_ref, wout_s):
    @pl.when(t < win_s.shape[0])
    def _():
        win_s[t] = win_ref[...].astype(BF16)

    rows = wout_ref.shape[0]

    @pl.when(t < wout_s.shape[0] // rows)
    def _():
        wout_s[pl.ds(pl.multiple_of(t * rows, rows), rows), :] = wout_ref[...].astype(BF16)


def _ab_kernel(x_ref, g_ref, win_ref, pw_ref, pb_ref, ps_ref, cw_ref, cb_ref, wout_ref,
               o_ref, win_s, wout_s, pool_hist, conv_hist, *, tm, n_load, tiles_per_seq):
    t = pl.program_id(0)

    @pl.when(t < n_load)
    def _():
        _load_weights(t, win_ref, win_s, wout_ref, wout_s)

    @pl.when(t >= n_load)
    def _():
        s = lax.rem(t - n_load, tiles_per_seq)

        @pl.when(s == 0)
        def _():
            pool_hist[...] = jnp.zeros_like(pool_hist)
            conv_hist[...] = jnp.zeros_like(conv_hist)

        n_sub = tm // AB_SUB
        proj = []
        for j in range(n_sub):
            hn = _rms(x_ref[j * AB_SUB:(j + 1) * AB_SUB, :], g_ref[...]).astype(BF16)
            proj.append([_dot(hn, win_s[k]) for k in range(4)])

        row = lax.broadcasted_iota(jnp.int32, (AB_SUB, POOL_GROUP), 0)
        pool_prev, conv_prev = pool_hist[...], conv_hist[...]
        for j in range(n_sub):
            rows = slice(j * AB_SUB, (j + 1) * AB_SUB)
            pu, sb, sc, sx = proj[j]

            frame = row + (s * tm + j * AB_SUB + 1)
            ext = jnp.concatenate([pool_prev, pu], axis=0)
            ya = []
            for gi, win in enumerate(POOL_WINDOWS):
                lo = gi * POOL_GROUP
                e = ext[:, lo:lo + POOL_GROUP]
                sh = 1
                while sh < win:
                    e = e + pltpu.roll(e, sh, axis=0)
                    sh *= 2
                ug = pu[:, lo:lo + POOL_GROUP]
                mean = e[POOL_HIST:] / jnp.minimum(frame, win).astype(F32)
                ya.append(_dot((mean - ug).astype(BF16), pw_ref[gi]))
            ya = (jnp.concatenate(ya, axis=-1) + pb_ref[...]) * ps_ref[...]

            z = sc * sx
            yb = sb * _causal_conv3(conv_prev, z, cw_ref[...], cb_ref[...])

            y = jnp.concatenate([ya, yb], axis=-1).astype(BF16)
            o_ref[rows, :] = x_ref[rows, :] + _dot(y, wout_s[...])
            pool_prev, conv_prev = pu[AB_SUB - POOL_HIST:], z[AB_SUB - CONV_HIST:]

        pool_hist[...] = pool_prev
        conv_hist[...] = conv_prev


def _chunk_cumsum(e):
    sh = 1
    while sh < CHUNK:
        e = e + jnp.concatenate([jnp.zeros((sh, e.shape[1]), F32), e[:CHUNK - sh]], axis=0)
        sh *= 2
    return e


def _zero_after(v):
    bits = lax.shift_right_logical(lax.shift_right_logical(pltpu.bitcast(v, jnp.int32), 16), 16)
    return bits.astype(F32)


def _gla_kernel(x_ref, g_ref, win_ref, wgl_ref, wg2_ref, bg_ref, ng_ref, wout_ref,
                o_ref, win_s, wout_s, state_ref, hn_s, q_s, k_s, v_s, gate_s, rg_s, y_s,
                *, tm, hk, hv, n_load, tiles_per_seq):
    t = pl.program_id(0)

    @pl.when(t < n_load)
    def _():
        _load_weights(t, win_ref, win_s, wout_ref, wout_s)

    @pl.when(t >= n_load)
    def _():
        @pl.when(lax.rem(t - n_load, tiles_per_seq) == 0)
        def _():
            state_ref[...] = jnp.zeros_like(state_ref)

        hn_s[...] = _rms(x_ref[...], g_ref[...]).astype(BF16)
        gl = _dot(hn_s[...], wgl_ref[...])
        gpre = _dot(gl.astype(BF16), wg2_ref[...]) + bg_ref[...]
        q_s[...] = _dot(hn_s[...], win_s[0]) * (hk ** -0.5)
        k = _dot(hn_s[...], win_s[1])
        k_s[...] = k
        gate_s[...] = (jnp.minimum(gpre, 0.0) - jnp.log(1.0 + jnp.exp(-jnp.abs(gpre)))) * (1.0 / GLA_TAU)
        anchors = [k[0:1, :]]
        for j in range(2):
            cols = slice(j * W_COLS, (j + 1) * W_COLS)
            v = _dot(hn_s[...], win_s[2 + j])
            v_s[:, cols] = v.astype(BF16)
            anchors.append(v[0:1, :])
        for j in range(2):
            cols = slice(j * W_COLS, (j + 1) * W_COLS)
            r = _dot(hn_s[...], win_s[4 + j])
            rg_s[:, cols] = r * jax.nn.sigmoid(r) * ng_ref[:, cols]
            if j == 0:
                anchors.append(r[0:1, :])

        ri = lax.broadcasted_iota(jnp.int32, (CHUNK, CHUNK), 0)
        ci = lax.broadcasted_iota(jnp.int32, (CHUNK, CHUNK), 1)
        causal = ri >= ci

        n_chunks = tm // CHUNK
        nt = (((1,), (1,)), ((), ()))
        tn = (((0,), (0,)), ((), ()))
        heads = [(slice(hd * hk, (hd + 1) * hk), slice(hd * hv, (hd + 1) * hv))
                 for hd in range(GLA_HEADS)]

        def chunk_prep(c):
            rows = slice(c * CHUNK, (c + 1) * CHUNK)
            gates = gate_s[rows, :] + _zero_after(anchors[c * len(anchors) // n_chunks])
            bcum = _chunk_cumsum(gates)
            blast = bcum[CHUNK - 1:CHUNK]
            kc = k_s[rows, :]
            q_in = (q_s[rows, :] * jnp.exp(bcum)).astype(BF16)
            k_in = (kc * jnp.exp(-bcum)).astype(BF16)
            k_out = (kc * jnp.exp(blast - bcum)).astype(BF16)
            decay = jnp.exp(blast)
            atts, dcols = [], []
            for ks, _ in heads:
                att = lax.dot_general(q_in[:, ks], k_in[:, ks], nt, preferred_element_type=F32)
                atts.append(jnp.where(causal, att, 0.0).astype(BF16))
                dcols.append(jnp.broadcast_to(decay[:, ks], (hk, hk)).T)
            return rows, q_in, k_out, dcols, atts

        states = [state_ref[hd] for hd in range(GLA_HEADS)]
        prep = chunk_prep(0)
        for c in range(n_chunks):
            nxt = chunk_prep(c + 1) if c + 1 < n_chunks else None
            rows, q_in, k_out, dcols, atts = prep
            kv = [lax.dot_general(k_out[:, ks], v_s[rows, vs], tn, preferred_element_type=F32)
                  for ks, vs in heads]
            for hd, (ks, vs) in enumerate(heads):
                lhs = jnp.concatenate([q_in[:, ks], atts[hd]], axis=1)
                rhs = jnp.concatenate([states[hd].astype(BF16), v_s[rows, vs]], axis=0)
                o = _dot(lhs, rhs)
                states[hd] = states[hd] * jnp.concatenate([dcols[hd]] * (hv // hk), axis=1) + kv[hd]
                o = o * lax.rsqrt(jnp.mean(o * o, axis=-1, keepdims=True) + EPS)
                y_s[rows, vs] = (o * rg_s[rows, vs]).astype(BF16)
            if (c + 1) % (n_chunks // 2) == 0:
                half = slice((c + 1 - n_chunks // 2) * CHUNK, (c + 1) * CHUNK)
                o_ref[half, :] = x_ref[half, :] + _dot(y_s[half, :], wout_s[...])
            prep = nxt

        for hd in range(GLA_HEADS):
            state_ref[hd] = states[hd]


def _ffn_kernel(x_ref, g_ref, wup_ref, cw_ref, cb_ref, wdn_ref, fg_ref, o_ref, wup_s, wdn_s,
                conv_hist, hn_s, a_s, *, tm, d_ff, final_norm, n_load, tiles_per_seq):
    t = pl.program_id(0)

    @pl.when(t < n_load)
    def _():
        _load_weights(t, wup_ref, wup_s, wdn_ref, wdn_s)

    @pl.when(t >= n_load)
    def _():
        @pl.when(lax.rem(t - n_load, tiles_per_seq) == 0)
        def _():
            conv_hist[...] = jnp.zeros_like(conv_hist)

        x = x_ref[...]
        hn_s[...] = _rms(x, g_ref[...]).astype(BF16)

        def up_proj(c0, width):
            pieces = []
            while width > 0:
                j, lo = divmod(c0, W_COLS)
                w = min(width, W_COLS - lo)
                pieces.append(_dot(hn_s[...], wup_s[j, :, lo:lo + w]))
                c0 += w
                width -= w
            return pieces[0] if len(pieces) == 1 else jnp.concatenate(pieces, axis=1)

        for c0 in range(0, d_ff, FF_CHUNK):
            cols = slice(c0, min(c0 + FF_CHUNK, d_ff))
            u = up_proj(c0, cols.stop - c0)
            v = up_proj(d_ff + c0, cols.stop - c0)
            uc = _causal_conv3(conv_hist[:, cols], u, cw_ref[:, cols], cb_ref[:, cols])
            a_s[:, cols] = (0.5 * uc * (1.0 + lax.erf(uc * SQRT_HALF)) * v).astype(BF16)
            conv_hist[:, cols] = u[tm - CONV_HIST:]

        out = x + _dot(a_s[...], wdn_s[...])
        if final_norm:
            out = _rms(out, fg_ref[...])
        o_ref[...] = out


def _const_spec(shape):
    nd = len(shape)
    return pl.BlockSpec(shape, lambda t: (0,) * nd, pipeline_mode=pl.Buffered(1))


def _col_chunk_spec(w, layer, lo, n):
    return pl.BlockSpec((None, w.shape[1], W_COLS), lambda t: (layer, 0, lo + jnp.minimum(t, n - 1)))


def _row_chunk_spec(w, layer, n):
    return pl.BlockSpec((None, W_ROWS, w.shape[2]), lambda t: (layer, jnp.minimum(t, n - 1), 0))


def _call(body, x, operands, in_specs, scratch, tm, n_load, name):
    bsz, seq, d = x.shape
    n_tiles = bsz * seq // tm
    tile_spec = pl.BlockSpec((tm, d), lambda t: (jnp.maximum(t - n_load, 0), 0))
    out = pl.pallas_call(
        functools.partial(body, tm=tm, n_load=n_load, tiles_per_seq=seq // tm),
        name=name,
        grid=(n_load + n_tiles,),
        in_specs=[tile_spec] + in_specs,
        out_specs=tile_spec,
        out_shape=jax.ShapeDtypeStruct((bsz * seq, d), x.dtype),
        scratch_shapes=scratch,
        compiler_params=pltpu.CompilerParams(
            dimension_semantics=("arbitrary",),
            vmem_limit_bytes=VMEM_LIMIT_BYTES),
    )(x.reshape(bsz * seq, d), *operands)
    return out.reshape(bsz, seq, d)


def _row(v):
    return v.reshape(1, -1).astype(F32)


def _ab_layer(x, i, norm_g, w_in, pool_w, pool_b, pool_scale, conv_w, conv_b, w_out):
    d = x.shape[-1]
    d_in, d_mid = w_in.shape[2], w_out.shape[1]
    assert pool_b.shape[0] == conv_b.shape[0] == W_COLS and d_in == 4 * W_COLS
    assert d_mid % W_ROWS == 0 and w_out.shape[2] == d
    n_in, n_out = d_in // W_COLS, d_mid // W_ROWS
    n_load = max(n_in, n_out)
    consts = [_row(norm_g), pool_w.astype(BF16), _row(pool_b), _row(pool_scale),
              conv_w.astype(F32), _row(conv_b)]
    operands = [consts[0], w_in] + consts[1:] + [w_out]
    in_specs = ([_const_spec(consts[0].shape), _col_chunk_spec(w_in, i, 0, n_in)]
                + [_const_spec(c.shape) for c in consts[1:]] + [_row_chunk_spec(w_out, i, n_out)])
    scratch = [pltpu.VMEM((n_in, d, W_COLS), BF16),
               pltpu.VMEM((d_mid, d), BF16),
               pltpu.VMEM((POOL_HIST, W_COLS), F32), pltpu.VMEM((CONV_HIST, W_COLS), F32)]
    return _call(_ab_kernel, x, operands, in_specs, scratch, AB_TILE, n_load, "pool_conv_mixer")


def _gla_layer(x, i, norm_g, w_in, w_g2, b_g, head_norm, w_out):
    d = x.shape[-1]
    dk = w_g2.shape[1]
    dv = w_out.shape[1]
    hk = dk // GLA_HEADS
    hv = dv // GLA_HEADS
    assert dk == W_COLS and dv == 2 * W_COLS and hv % hk == 0 and dv % W_ROWS == 0
    n_in, n_out = (2 * dk + 2 * dv) // W_COLS, dv // W_ROWS
    n_load = max(n_in, n_out)
    wgl = jnp.pad(w_in[i, :, 2 * dk + 2 * dv:], ((0, 0), (0, LANES - GLA_RANK)))
    wg2 = jnp.pad(w_g2, ((0, LANES - GLA_RANK), (0, 0)))
    consts = [wgl.astype(BF16), wg2.astype(BF16), _row(b_g), _row(jnp.tile(head_norm, GLA_HEADS))]
    operands = [_row(norm_g), w_in] + consts + [w_out]
    in_specs = ([_const_spec((1, d)), _col_chunk_spec(w_in, i, 0, n_in)]
                + [_const_spec(c.shape) for c in consts] + [_row_chunk_spec(w_out, i, n_out)])
    tm = GLA_TILE
    scratch = [pltpu.VMEM((n_in, d, W_COLS), BF16),
               pltpu.VMEM((dv, d), BF16),
               pltpu.VMEM((GLA_HEADS, hk, hv), F32),
               pltpu.VMEM((tm, d), BF16),
               pltpu.VMEM((tm, dk), F32),
               pltpu.VMEM((tm, dk), F32),
               pltpu.VMEM((tm, dv), BF16),
               pltpu.VMEM((tm, dk), F32),
               pltpu.VMEM((tm, dv), F32),
               pltpu.VMEM((tm, dv), BF16)]
    body = functools.partial(_gla_kernel, hk=hk, hv=hv)
    return _call(body, x, operands, in_specs, scratch, tm, n_load, "gla_mixer")


def _ffn_layer(x, l, norm_g, w_up, conv_w, conv_b, w_down, final_g, final_norm):
    d = x.shape[-1]
    d_ff = w_down.shape[1]
    assert d_ff % MXU_COLS == 0 and (2 * d_ff) % W_COLS == 0 and d_ff % W_ROWS == 0
    n_in, n_out = 2 * d_ff // W_COLS, d_ff // W_ROWS
    n_load = max(n_in, n_out)
    consts = [conv_w.astype(F32), _row(conv_b)]
    operands = [_row(norm_g), w_up] + consts + [w_down, _row(final_g)]
    in_specs = ([_const_spec((1, d)), _col_chunk_spec(w_up, l, 0, n_in)]
                + [_const_spec(c.shape) for c in consts]
                + [_row_chunk_spec(w_down, l, n_out), _const_spec((1, d))])
    tm = FFN_TILE
    scratch = [pltpu.VMEM((n_in, d, W_COLS), BF16),
               pltpu.VMEM((d_ff, d), BF16),
               pltpu.VMEM((CONV_HIST, d_ff), F32),
               pltpu.VMEM((tm, d), BF16),
               pltpu.VMEM((tm, d_ff), BF16)]
    body = functools.partial(_ffn_kernel, d_ff=d_ff, final_norm=final_norm)
    return _call(body, x, operands, in_specs, scratch, tm, n_load, "convffn")


def kernel(x, mix_norm, ffn_norm, ab_w_in, pool_w, pool_b, pool_scale, sc_conv_w, sc_conv_b,
           ab_w_out, gla_w_in, gla_w_g2, gla_b_g, gla_norm, gla_w_out, ffn_w_up, ffn_conv_w,
           ffn_conv_b, ffn_w_down, final_norm):
    depth = mix_norm.shape[0]
    seq = x.shape[1]
    assert seq % AB_TILE == 0 and seq % GLA_TILE == 0 and seq % FFN_TILE == 0 and GLA_TILE % CHUNK == 0
    for l in range(depth):
        i = l // 2
        if l % 2 == 0:
            x = _ab_layer(x, i, mix_norm[l], ab_w_in, pool_w[i], pool_b[i], pool_scale[i],
                          sc_conv_w[i], sc_conv_b[i], ab_w_out)
        else:
            x = _gla_layer(x, i, mix_norm[l], gla_w_in, gla_w_g2[i], gla_b_g[i], gla_norm[i],
                           gla_w_out)
        x = _ffn_layer(x, l, ffn_norm[l], ffn_w_up, ffn_conv_w[l], ffn_conv_b[l], ffn_w_down,
                       final_norm, final_norm=(l == depth - 1))
    return x
```

```python
import functools
import math

import jax
import jax.numpy as jnp
from jax import lax
from jax.experimental import pallas as pl
from jax.experimental.pallas import tpu as pltpu

F32 = jnp.float32
BF16 = jnp.bfloat16

EPS = 1e-6
CHUNK = 64
POOL_WINDOWS = (2, 4, 8, 16)
POOL_GROUP = 128
CONV_WIDTH = 3
GLA_HEADS = 4
GLA_RANK = 16
GLA_TAU = 16.0

LANES = 128
SUBLANES = 8
MXU_COLS = 256
POOL_HIST = 16
CONV_HIST = SUBLANES
FF_CHUNK = 4 * MXU_COLS
W_COLS = 2 * MXU_COLS
W_ROWS = MXU_COLS
AB_TILE = 2048
AB_SUB = 512
GLA_TILE = 1024
GLA_OUT_CHUNKS = 8
FFN_TILE = 1024
VMEM_LIMIT_BYTES = 56 * 1024 * 1024

SQRT_HALF = math.sqrt(0.5)


def _rms(x, g):
    ms = jnp.mean(x * x, axis=-1, keepdims=True)
    return x * lax.rsqrt(ms + EPS) * g


def _shifted(hist, cur, shift):
    ext = jnp.concatenate([hist, cur], axis=0)
    return pltpu.roll(ext, shift, axis=0)[hist.shape[0]:]


def _causal_conv3(hist, u, w, b):
    u1 = _shifted(hist, u, 1)
    u2 = _shifted(hist, u, 2)
    return b + w[0:1] * u2 + w[1:2] * u1 + w[2:3] * u


def _dot(a, b):
    return jnp.dot(a, b, preferred_element_type=F32)


def _load_weights(t, win_ref, win_s, wout_ref, wout_s):
    @pl.when(t < win_s.shape[0])
    def _():
        win_s[t] = win_ref[...].astype(BF16)

    rows = wout_ref.shape[0]

    @pl.when(t < wout_s.shape[0] // rows)
    def _():
        wout_s[pl.ds(pl.multiple_of(t * rows, rows), rows), :] = wout_ref[...].astype(BF16)


def _ab_kernel(x_ref, g_ref, win_ref, pw_ref, pb_ref, ps_ref, cw_ref, cb_ref, wout_ref,
               o_ref, win_s, wout_s, pool_hist, conv_hist, *, tm, n_load, tiles_per_seq):
    t = pl.program_id(0)

    @pl.when(t < n_load)
    def _():
        _load_weights(t, win_ref, win_s, wout_ref, wout_s)

    @pl.when(t >= n_load)
    def _():
        s = lax.rem(t - n_load, tiles_per_seq)

        @pl.when(s == 0)
        def _():
            pool_hist[...] = jnp.zeros_like(pool_hist)
            conv_hist[...] = jnp.zeros_like(conv_hist)

        n_sub = tm // AB_SUB
        proj = []
        for j in range(n_sub):
            hn = _rms(x_ref[j * AB_SUB:(j + 1) * AB_SUB, :], g_ref[...]).astype(BF16)
            proj.append([_dot(hn, win_s[k]) for k in range(4)])

        row = lax.broadcasted_iota(jnp.int32, (AB_SUB, POOL_GROUP), 0)
        pool_prev, conv_prev = pool_hist[...], conv_hist[...]
        for j in range(n_sub):
            rows = slice(j * AB_SUB, (j + 1) * AB_SUB)
            pu, sb, sc, sx = proj[j]

            frame = row + (s * tm + j * AB_SUB + 1)
            ext = jnp.concatenate([pool_prev, pu], axis=0)
            ya = []
            for gi, win in enumerate(POOL_WINDOWS):
                lo = gi * POOL_GROUP
                e = ext[:, lo:lo + POOL_GROUP]
                sh = 1
                while sh < win:
                    e = e + pltpu.roll(e, sh, axis=0)
                    sh *= 2
                ug = pu[:, lo:lo + POOL_GROUP]
                mean = e[POOL_HIST:] / jnp.minimum(frame, win).astype(F32)
                ya.append(_dot((mean - ug).astype(BF16), pw_ref[gi]))
            ya = (jnp.concatenate(ya, axis=-1) + pb_ref[...]) * ps_ref[...]

            z = sc * sx
            yb = sb * _causal_conv3(conv_prev, z, cw_ref[...], cb_ref[...])

            y = jnp.concatenate([ya, yb], axis=-1).astype(BF16)
            o_ref[rows, :] = x_ref[rows, :] + _dot(y, wout_s[...])
            pool_prev, conv_prev = pu[AB_SUB - POOL_HIST:], z[AB_SUB - CONV_HIST:]

        pool_hist[...] = pool_prev
        conv_hist[...] = conv_prev


def _chunk_cumsum(e):
    sh = 1
    while sh < CHUNK:
        e = e + jnp.concatenate([jnp.zeros((sh, e.shape[1]), F32), e[:CHUNK - sh]], axis=0)
        sh *= 2
    return e


def _zero_after(v):
    bits = lax.shift_right_logical(lax.shift_right_logical(pltpu.bitcast(v, jnp.int32), 16), 16)
    return bits.astype(F32)


def _gla_kernel(x_ref, g_ref, win_ref, wgl_ref, wg2_ref, bg_ref, ng_ref, wout_ref,
                o_ref, win_s, wout_s, state_ref, hn_s, q_s, k_s, v_s, gate_s, rg_s, y_s,
                *, tm, hk, hv, n_load, tiles_per_seq):
    t = pl.program_id(0)

    @pl.when(t < n_load)
    def _():
        _load_weights(t, win_ref, win_s, wout_ref, wout_s)

    @pl.when(t >= n_load)
    def _():
        @pl.when(lax.rem(t - n_load, tiles_per_seq) == 0)
        def _():
            state_ref[...] = jnp.zeros_like(state_ref)

        hn_s[...] = _rms(x_ref[...], g_ref[...]).astype(BF16)
        gl = _dot(hn_s[...], wgl_ref[...])
        gpre = _dot(gl.astype(BF16), wg2_ref[...]) + bg_ref[...]
        q_s[...] = _dot(hn_s[...], win_s[0]) * (hk ** -0.5)
        k = _dot(hn_s[...], win_s[1])
        k_s[...] = k
        gate_s[...] = (jnp.minimum(gpre, 0.0) - jnp.log(1.0 + jnp.exp(-jnp.abs(gpre)))) * (1.0 / GLA_TAU)
        anchors = [k[0:1, :]]
        for j in range(2):
            cols = slice(j * W_COLS, (j + 1) * W_COLS)
            v = _dot(hn_s[...], win_s[2 + j])
            v_s[:, cols] = v.astype(BF16)
            anchors.append(v[0:1, :])
        for j in range(2):
            cols = slice(j * W_COLS, (j + 1) * W_COLS)
            r = _dot(hn_s[...], win_s[4 + j])
            rg_s[:, cols] = r * jax.nn.sigmoid(r) * ng_ref[:, cols]
            if j == 0:
                anchors.append(r[0:1, :])

        ri = lax.broadcasted_iota(jnp.int32, (CHUNK, CHUNK), 0)
        ci = lax.broadcasted_iota(jnp.int32, (CHUNK, CHUNK), 1)
        causal = ri >= ci

        n_chunks = tm // CHUNK
        nt = (((1,), (1,)), ((), ()))
        tn = (((0,), (0,)), ((), ()))
        heads = [(slice(hd * hk, (hd + 1) * hk), slice(hd * hv, (hd + 1) * hv))
                 for hd in range(GLA_HEADS)]

        def chunk_prep(c):
            rows = slice(c * CHUNK, (c + 1) * CHUNK)
            gates = gate_s[rows, :] + _zero_after(anchors[c * len(anchors) // n_chunks])
            bcum = _chunk_cumsum(gates)
            blast = bcum[CHUNK - 1:CHUNK]
            kc = k_s[rows, :]
            q_in = (q_s[rows, :] * jnp.exp(bcum)).astype(BF16)
            k_in = (kc * jnp.exp(-bcum)).astype(BF16)
            k_out = (kc * jnp.exp(blast - bcum)).astype(BF16)
            decay = jnp.exp(blast)
            atts, dcols = [], []
            for ks, _ in heads:
                att = lax.dot_general(q_in[:, ks], k_in[:, ks], nt, preferred_element_type=F32)
                atts.append(jnp.where(causal, att, 0.0).astype(BF16))
                dcols.append(jnp.broadcast_to(decay[:, ks], (hk, hk)).T)
            return rows, q_in, k_out, dcols, atts

        states = [state_ref[hd] for hd in range(GLA_HEADS)]
        prep = chunk_prep(0)
        for c in range(n_chunks):
            nxt = chunk_prep(c + 1) if c + 1 < n_chunks else None
            rows, q_in, k_out, dcols, atts = prep
            kv = [lax.dot_general(k_out[:, ks], v_s[rows, vs], tn, preferred_element_type=F32)
                  for ks, vs in heads]
            for hd, (ks, vs) in enumerate(heads):
                lhs = jnp.concatenate([q_in[:, ks], atts[hd]], axis=1)
                rhs = jnp.concatenate([states[hd].astype(BF16), v_s[rows, vs]], axis=0)
                o = _dot(lhs, rhs)
                states[hd] = states[hd] * jnp.concatenate([dcols[hd]] * (hv // hk), axis=1) + kv[hd]
                o = o * lax.rsqrt(jnp.mean(o * o, axis=-1, keepdims=True) + EPS)
                y_s[rows, vs] = (o * rg_s[rows, vs]).astype(BF16)
            if (c + 1) % GLA_OUT_CHUNKS == 0:
                done = slice((c + 1 - GLA_OUT_CHUNKS) * CHUNK, (c + 1) * CHUNK)
                o_ref[done, :] = x_ref[done, :] + _dot(y_s[done, :], wout_s[...])
            prep = nxt

        for hd in range(GLA_HEADS):
            state_ref[hd] = states[hd]


def _ffn_kernel(x_ref, g_ref, wup_ref, cw_ref, cb_ref, wdn_ref, fg_ref, o_ref, wup_s, wdn_s,
                conv_hist, hn_s, a_s, *, tm, d_ff, final_norm, n_load, tiles_per_seq):
    t = pl.program_id(0)

    @pl.when(t < n_load)
    def _():
        _load_weights(t, wup_ref, wup_s, wdn_ref, wdn_s)

    @pl.when(t >= n_load)
    def _():
        @pl.when(lax.rem(t - n_load, tiles_per_seq) == 0)
        def _():
            conv_hist[...] = jnp.zeros_like(conv_hist)

        x = x_ref[...]
        hn_s[...] = _rms(x, g_ref[...]).astype(BF16)

        def up_proj(c0, width):
            pieces = []
            while width > 0:
                j, lo = divmod(c0, W_COLS)
                w = min(width, W_COLS - lo)
                pieces.append(_dot(hn_s[...], wup_s[j, :, lo:lo + w]))
                c0 += w
                width -= w
            return pieces[0] if len(pieces) == 1 else jnp.concatenate(pieces, axis=1)

        for c0 in range(0, d_ff, FF_CHUNK):
            cols = slice(c0, min(c0 + FF_CHUNK, d_ff))
            u = up_proj(c0, cols.stop - c0)
            v = up_proj(d_ff + c0, cols.stop - c0)
            uc = _causal_conv3(conv_hist[:, cols], u, cw_ref[:, cols], cb_ref[:, cols])
            a_s[:, cols] = (0.5 * uc * (1.0 + lax.erf(uc * SQRT_HALF)) * v).astype(BF16)
            conv_hist[:, cols] = u[tm - CONV_HIST:]

        out = x + _dot(a_s[...], wdn_s[...])
        if final_norm:
            out = _rms(out, fg_ref[...])
        o_ref[...] = out


def _const_spec(shape):
    nd = len(shape)
    return pl.BlockSpec(shape, lambda t: (0,) * nd, pipeline_mode=pl.Buffered(1))


def _col_chunk_spec(w, layer, lo, n):
    return pl.BlockSpec((None, w.shape[1], W_COLS), lambda t: (layer, 0, lo + jnp.minimum(t, n - 1)))


def _row_chunk_spec(w, layer, n):
    return pl.BlockSpec((None, W_ROWS, w.shape[2]), lambda t: (layer, jnp.minimum(t, n - 1), 0))


def _call(body, x, operands, in_specs, scratch, tm, n_load, name):
    bsz, seq, d = x.shape
    n_tiles = bsz * seq // tm
    tile_spec = pl.BlockSpec((tm, d), lambda t: (jnp.maximum(t - n_load, 0), 0))
    out = pl.pallas_call(
        functools.partial(body, tm=tm, n_load=n_load, tiles_per_seq=seq // tm),
        name=name,
        grid=(n_load + n_tiles,),
        in_specs=[tile_spec] + in_specs,
        out_specs=tile_spec,
        out_shape=jax.ShapeDtypeStruct((bsz * seq, d), x.dtype),
        scratch_shapes=scratch,
        compiler_params=pltpu.CompilerParams(
            dimension_semantics=("arbitrary",),
            vmem_limit_bytes=VMEM_LIMIT_BYTES),
    )(x.reshape(bsz * seq, d), *operands)
    return out.reshape(bsz, seq, d)


def _row(v):
    return v.reshape(1, -1).astype(F32)


def _ab_layer(x, i, norm_g, w_in, pool_w, pool_b, pool_scale, conv_w, conv_b, w_out):
    d = x.shape[-1]
    d_in, d_mid = w_in.shape[2], w_out.shape[1]
    assert pool_b.shape[0] == conv_b.shape[0] == W_COLS and d_in == 4 * W_COLS
    assert d_mid % W_ROWS == 0 and w_out.shape[2] == d
    n_in, n_out = d_in // W_COLS, d_mid // W_ROWS
    n_load = max(n_in, n_out)
    consts = [_row(norm_g), pool_w.astype(BF16), _row(pool_b), _row(pool_scale),
              conv_w.astype(F32), _row(conv_b)]
    operands = [consts[0], w_in] + consts[1:] + [w_out]
    in_specs = ([_const_spec(consts[0].shape), _col_chunk_spec(w_in, i, 0, n_in)]
                + [_const_spec(c.shape) for c in consts[1:]] + [_row_chunk_spec(w_out, i, n_out)])
    scratch = [pltpu.VMEM((n_in, d, W_COLS), BF16),
               pltpu.VMEM((d_mid, d), BF16),
               pltpu.VMEM((POOL_HIST, W_COLS), F32), pltpu.VMEM((CONV_HIST, W_COLS), F32)]
    return _call(_ab_kernel, x, operands, in_specs, scratch, AB_TILE, n_load, "pool_conv_mixer")


def _gla_layer(x, i, norm_g, w_in, w_g2, b_g, head_norm, w_out):
    d = x.shape[-1]
    dk = w_g2.shape[1]
    dv = w_out.shape[1]
    hk = dk // GLA_HEADS
    hv = dv // GLA_HEADS
    assert dk == W_COLS and dv == 2 * W_COLS and hv % hk == 0 and dv % W_ROWS == 0
    n_in, n_out = (2 * dk + 2 * dv) // W_COLS, dv // W_ROWS
    n_load = max(n_in, n_out)
    wgl = jnp.pad(w_in[i, :, 2 * dk + 2 * dv:], ((0, 0), (0, LANES - GLA_RANK)))
    wg2 = jnp.pad(w_g2, ((0, LANES - GLA_RANK), (0, 0)))
    consts = [wgl.astype(BF16), wg2.astype(BF16), _row(b_g), _row(jnp.tile(head_norm, GLA_HEADS))]
    operands = [_row(norm_g), w_in] + consts + [w_out]
    in_specs = ([_const_spec((1, d)), _col_chunk_spec(w_in, i, 0, n_in)]
                + [_const_spec(c.shape) for c in consts] + [_row_chunk_spec(w_out, i, n_out)])
    tm = GLA_TILE
    scratch = [pltpu.VMEM((n_in, d, W_COLS), BF16),
               pltpu.VMEM((dv, d), BF16),
               pltpu.VMEM((GLA_HEADS, hk, hv), F32),
               pltpu.VMEM((tm, d), BF16),
               pltpu.VMEM((tm, dk), F32),
               pltpu.VMEM((tm, dk), F32),
               pltpu.VMEM((tm, dv), BF16),
               pltpu.VMEM((tm, dk), F32),
               pltpu.VMEM((tm, dv), F32),
               pltpu.VMEM((tm, dv), BF16)]
    body = functools.partial(_gla_kernel, hk=hk, hv=hv)
    return _call(body, x, operands, in_specs, scratch, tm, n_load, "gla_mixer")


def _ffn_layer(x, l, norm_g, w_up, conv_w, conv_b, w_down, final_g, final_norm):
    d = x.shape[-1]
    d_ff = w_down.shape[1]
    assert d_ff % MXU_COLS == 0 and (2 * d_ff) % W_COLS == 0 and d_ff % W_ROWS == 0
    n_in, n_out = 2 * d_ff // W_COLS, d_ff // W_ROWS
    n_load = max(n_in, n_out)
    consts = [conv_w.astype(F32), _row(conv_b)]
    operands = [_row(norm_g), w_up] + consts + [w_down, _row(final_g)]
    in_specs = ([_const_spec((1, d)), _col_chunk_spec(w_up, l, 0, n_in)]
                + [_const_spec(c.shape) for c in consts]
                + [_row_chunk_spec(w_down, l, n_out), _const_spec((1, d))])
    tm = FFN_TILE
    scratch = [pltpu.VMEM((n_in, d, W_COLS), BF16),
               pltpu.VMEM((d_ff, d), BF16),
               pltpu.VMEM((CONV_HIST, d_ff), F32),
               pltpu.VMEM((tm, d), BF16),
               pltpu.VMEM((tm, d_ff), BF16)]
    body = functools.partial(_ffn_kernel, d_ff=d_ff, final_norm=final_norm)
    return _call(body, x, operands, in_specs, scratch, tm, n_load, "convffn")


def kernel(x, mix_norm, ffn_norm, ab_w_in, pool_w, pool_b, pool_scale, sc_conv_w, sc_conv_b,
           ab_w_out, gla_w_in, gla_w_g2, gla_b_g, gla_norm, gla_w_out, ffn_w_up, ffn_conv_w,
           ffn_conv_b, ffn_w_down, final_norm):
    depth = mix_norm.shape[0]
    seq = x.shape[1]
    assert seq % AB_TILE == 0 and seq % GLA_TILE == 0 and seq % FFN_TILE == 0 and GLA_TILE % CHUNK == 0
    for l in range(depth):
        i = l // 2
        if l % 2 == 0:
            x = _ab_layer(x, i, mix_norm[l], ab_w_in, pool_w[i], pool_b[i], pool_scale[i],
                          sc_conv_w[i], sc_conv_b[i], ab_w_out)
        else:
            x = _gla_layer(x, i, mix_norm[l], gla_w_in, gla_w_g2[i], gla_b_g[i], gla_norm[i],
                           gla_w_out)
        x = _ffn_layer(x, l, ffn_norm[l], ffn_w_up, ffn_conv_w[l], ffn_conv_b[l], ffn_w_down,
                       final_norm, final_norm=(l == depth - 1))
    return x
```

```python
import functools
import math

import jax
import jax.numpy as jnp
from jax import lax
from jax.experimental import pallas as pl
from jax.experimental.pallas import tpu as pltpu

F32 = jnp.float32
BF16 = jnp.bfloat16

EPS = 1e-6
CHUNK = 64
POOL_WINDOWS = (2, 4, 8, 16)
POOL_GROUP = 128
CONV_WIDTH = 3
GLA_HEADS = 4
GLA_RANK = 16
GLA_TAU = 16.0

LANES = 128
SUBLANES = 8
MXU_COLS = 256
POOL_HIST = 16
CONV_HIST = SUBLANES
FF_CHUNK = 4 * MXU_COLS
W_COLS = 2 * MXU_COLS
W_ROWS = MXU_COLS
AB_TILE = 2048
AB_SUB = 512
GLA_TILE = 1024
GLA_SUB = 256
GLA_OUT_CHUNKS = 8
FFN_TILE = 1024
VMEM_LIMIT_BYTES = 56 * 1024 * 1024

SQRT_HALF = math.sqrt(0.5)


def _rms(x, g):
    ms = jnp.mean(x * x, axis=-1, keepdims=True)
    return x * lax.rsqrt(ms + EPS) * g


def _shifted(hist, cur, shift):
    ext = jnp.concatenate([hist, cur], axis=0)
    return pltpu.roll(ext, shift, axis=0)[hist.shape[0]:]


def _causal_conv3(hist, u, w, b):
    u1 = _shifted(hist, u, 1)
    u2 = _shifted(hist, u, 2)
    return b + w[0:1] * u2 + w[1:2] * u1 + w[2:3] * u


def _dot(a, b):
    return jnp.dot(a, b, preferred_element_type=F32)


def _load_weights(t, win_ref, win_s, wout_ref, wout_s):
    @pl.when(t < win_s.shape[0])
    def _():
        win_s[t] = win_ref[...].astype(BF16)

    rows = wout_ref.shape[0]

    @pl.when(t < wout_s.shape[0] // rows)
    def _():
        wout_s[pl.ds(pl.multiple_of(t * rows, rows), rows), :] = wout_ref[...].astype(BF16)


def _ab_kernel(x_ref, g_ref, win_ref, pw_ref, pb_ref, ps_ref, cw_ref, cb_ref, wout_ref,
               o_ref, win_s, wout_s, pool_hist, conv_hist, *, tm, n_load, tiles_per_seq):
    t = pl.program_id(0)

    @pl.when(t < n_load)
    def _():
        _load_weights(t, win_ref, win_s, wout_ref, wout_s)

    @pl.when(t >= n_load)
    def _():
        s = lax.rem(t - n_load, tiles_per_seq)

        @pl.when(s == 0)
        def _():
            pool_hist[...] = jnp.zeros_like(pool_hist)
            conv_hist[...] = jnp.zeros_like(conv_hist)

        n_sub = tm // AB_SUB
        proj = []
        for j in range(n_sub):
            hn = _rms(x_ref[j * AB_SUB:(j + 1) * AB_SUB, :], g_ref[...]).astype(BF16)
            proj.append([_dot(hn, win_s[k]) for k in range(4)])

        row = lax.broadcasted_iota(jnp.int32, (AB_SUB, POOL_GROUP), 0)
        pool_prev, conv_prev = pool_hist[...], conv_hist[...]
        for j in range(n_sub):
            rows = slice(j * AB_SUB, (j + 1) * AB_SUB)
            pu, sb, sc, sx = proj[j]

            frame = row + (s * tm + j * AB_SUB + 1)
            ext = jnp.concatenate([pool_prev, pu], axis=0)
            ya = []
            for gi, win in enumerate(POOL_WINDOWS):
                lo = gi * POOL_GROUP
                e = ext[:, lo:lo + POOL_GROUP]
                sh = 1
                while sh < win:
                    e = e + pltpu.roll(e, sh, axis=0)
                    sh *= 2
                ug = pu[:, lo:lo + POOL_GROUP]
                mean = e[POOL_HIST:] / jnp.minimum(frame, win).astype(F32)
                ya.append(_dot((mean - ug).astype(BF16), pw_ref[gi]))
            ya = (jnp.concatenate(ya, axis=-1) + pb_ref[...]) * ps_ref[...]

            z = sc * sx
            yb = sb * _causal_conv3(conv_prev, z, cw_ref[...], cb_ref[...])

            y = jnp.concatenate([ya, yb], axis=-1).astype(BF16)
            o_ref[rows, :] = x_ref[rows, :] + _dot(y, wout_s[...])
            pool_prev, conv_prev = pu[AB_SUB - POOL_HIST:], z[AB_SUB - CONV_HIST:]

        pool_hist[...] = pool_prev
        conv_hist[...] = conv_prev


def _chunk_cumsum(e):
    sh = 1
    while sh < CHUNK:
        e = e + jnp.concatenate([jnp.zeros((sh, e.shape[1]), F32), e[:CHUNK - sh]], axis=0)
        sh *= 2
    return e


def _zero_after(v):
    bits = lax.shift_right_logical(lax.shift_right_logical(pltpu.bitcast(v, jnp.int32), 16), 16)
    return bits.astype(F32)


def _gla_kernel(x_ref, g_ref, win_ref, wgl_ref, wg2_ref, bg_ref, ng_ref, wout_ref,
                o_ref, win_s, wout_s, state_ref, hn_s, q_s, k_s, v_s, gate_s, rg_s, y_s,
                *, tm, hk, hv, n_load, tiles_per_seq):
    t = pl.program_id(0)

    @pl.when(t < n_load)
    def _():
        _load_weights(t, win_ref, win_s, wout_ref, wout_s)

    @pl.when(t >= n_load)
    def _():
        @pl.when(lax.rem(t - n_load, tiles_per_seq) == 0)
        def _():
            state_ref[...] = jnp.zeros_like(state_ref)

        anchors = []
        for i in range(tm // GLA_SUB):
            rs = slice(i * GLA_SUB, (i + 1) * GLA_SUB)
            hn_s[rs, :] = _rms(x_ref[rs, :], g_ref[...]).astype(BF16)
            gl = _dot(hn_s[rs, :], wgl_ref[...])
            gpre = _dot(gl.astype(BF16), wg2_ref[...]) + bg_ref[...]
            q_s[rs, :] = _dot(hn_s[rs, :], win_s[0]) * (hk ** -0.5)
            k = _dot(hn_s[rs, :], win_s[1])
            k_s[rs, :] = k
            gate_s[rs, :] = ((jnp.minimum(gpre, 0.0) - jnp.log(1.0 + jnp.exp(-jnp.abs(gpre))))
                             * (1.0 / GLA_TAU))
            sub_anchors = [k[0:1, :]]
            for j in range(2):
                cols = slice(j * W_COLS, (j + 1) * W_COLS)
                v = _dot(hn_s[rs, :], win_s[2 + j])
                v_s[rs, cols] = v.astype(BF16)
                sub_anchors.append(v[0:1, :])
            for j in range(2):
                cols = slice(j * W_COLS, (j + 1) * W_COLS)
                r = _dot(hn_s[rs, :], win_s[4 + j])
                rg_s[rs, cols] = r * jax.nn.sigmoid(r) * ng_ref[:, cols]
                if j == 0:
                    sub_anchors.append(r[0:1, :])
            anchors.append(sub_anchors)

        ri = lax.broadcasted_iota(jnp.int32, (CHUNK, CHUNK), 0)
        ci = lax.broadcasted_iota(jnp.int32, (CHUNK, CHUNK), 1)
        causal = ri >= ci

        n_chunks = tm // CHUNK
        nt = (((1,), (1,)), ((), ()))
        tn = (((0,), (0,)), ((), ()))
        heads = [(slice(hd * hk, (hd + 1) * hk), slice(hd * hv, (hd + 1) * hv))
                 for hd in range(GLA_HEADS)]

        def chunk_prep(c):
            rows = slice(c * CHUNK, (c + 1) * CHUNK)
            sub, c_sub = divmod(c, GLA_SUB // CHUNK)
            anchor = anchors[sub][c_sub * len(anchors[sub]) // (GLA_SUB // CHUNK)]
            gates = gate_s[rows, :] + _zero_after(anchor)
            bcum = _chunk_cumsum(gates)
            blast = bcum[CHUNK - 1:CHUNK]
            kc = k_s[rows, :]
            q_in = (q_s[rows, :] * jnp.exp(bcum)).astype(BF16)
            k_in = (kc * jnp.exp(-bcum)).astype(BF16)
            k_out = (kc * jnp.exp(blast - bcum)).astype(BF16)
            decay = jnp.exp(blast)
            atts, dcols = [], []
            for ks, _ in heads:
                att = lax.dot_general(q_in[:, ks], k_in[:, ks], nt, preferred_element_type=F32)
                atts.append(jnp.where(causal, att, 0.0).astype(BF16))
                dcols.append(jnp.broadcast_to(decay[:, ks], (hk, hk)).T)
            return rows, q_in, k_out, dcols, atts

        states = [state_ref[hd] for hd in range(GLA_HEADS)]
        prep = chunk_prep(0)
        for c in range(n_chunks):
            nxt = chunk_prep(c + 1) if c + 1 < n_chunks else None
            rows, q_in, k_out, dcols, atts = prep
            kv = [lax.dot_general(k_out[:, ks], v_s[rows, vs], tn, preferred_element_type=F32)
                  for ks, vs in heads]
            for hd, (ks, vs) in enumerate(heads):
                lhs = jnp.concatenate([q_in[:, ks], atts[hd]], axis=1)
                rhs = jnp.concatenate([states[hd].astype(BF16), v_s[rows, vs]], axis=0)
                o = _dot(lhs, rhs)
                states[hd] = states[hd] * jnp.concatenate([dcols[hd]] * (hv // hk), axis=1) + kv[hd]
                o = o * lax.rsqrt(jnp.mean(o * o, axis=-1, keepdims=True) + EPS)
                y_s[rows, vs] = (o * rg_s[rows, vs]).astype(BF16)
            if (c + 1) % GLA_OUT_CHUNKS == 0:
                done = slice((c + 1 - GLA_OUT_CHUNKS) * CHUNK, (c + 1) * CHUNK)
                o_ref[done, :] = x_ref[done, :] + _dot(y_s[done, :], wout_s[...])
            prep = nxt

        for hd in range(GLA_HEADS):
            state_ref[hd] = states[hd]


def _ffn_kernel(x_ref, g_ref, wup_ref, cw_ref, cb_ref, wdn_ref, fg_ref, o_ref, wup_s, wdn_s,
                conv_hist, hn_s, a_s, *, tm, d_ff, final_norm, n_load, tiles_per_seq):
    t = pl.program_id(0)

    @pl.when(t < n_load)
    def _():
        _load_weights(t, wup_ref, wup_s, wdn_ref, wdn_s)

    @pl.when(t >= n_load)
    def _():
        @pl.when(lax.rem(t - n_load, tiles_per_seq) == 0)
        def _():
            conv_hist[...] = jnp.zeros_like(conv_hist)

        x = x_ref[...]
        hn_s[...] = _rms(x, g_ref[...]).astype(BF16)

        def up_proj(c0, width):
            pieces = []
            while width > 0:
                j, lo = divmod(c0, W_COLS)
                w = min(width, W_COLS - lo)
                pieces.append(_dot(hn_s[...], wup_s[j, :, lo:lo + w]))
                c0 += w
                width -= w
            return pieces[0] if len(pieces) == 1 else jnp.concatenate(pieces, axis=1)

        for c0 in range(0, d_ff, FF_CHUNK):
            cols = slice(c0, min(c0 + FF_CHUNK, d_ff))
            u = up_proj(c0, cols.stop - c0)
            v = up_proj(d_ff + c0, cols.stop - c0)
            uc = _causal_conv3(conv_hist[:, cols], u, cw_ref[:, cols], cb_ref[:, cols])
            a_s[:, cols] = (0.5 * uc * (1.0 + lax.erf(uc * SQRT_HALF)) * v).astype(BF16)
            conv_hist[:, cols] = u[tm - CONV_HIST:]

        out = x + _dot(a_s[...], wdn_s[...])
        if final_norm:
            out = _rms(out, fg_ref[...])
        o_ref[...] = out


def _const_spec(shape):
    nd = len(shape)
    return pl.BlockSpec(shape, lambda t: (0,) * nd, pipeline_mode=pl.Buffered(1))


def _col_chunk_spec(w, layer, lo, n):
    return pl.BlockSpec((None, w.shape[1], W_COLS), lambda t: (layer, 0, lo + jnp.minimum(t, n - 1)))


def _row_chunk_spec(w, layer, n):
    return pl.BlockSpec((None, W_ROWS, w.shape[2]), lambda t: (layer, jnp.minimum(t, n - 1), 0))


def _call(body, x, operands, in_specs, scratch, tm, n_load, name):
    bsz, seq, d = x.shape
    n_tiles = bsz * seq // tm
    tile_spec = pl.BlockSpec((tm, d), lambda t: (jnp.maximum(t - n_load, 0), 0))
    out = pl.pallas_call(
        functools.partial(body, tm=tm, n_load=n_load, tiles_per_seq=seq // tm),
        name=name,
        grid=(n_load + n_tiles,),
        in_specs=[tile_spec] + in_specs,
        out_specs=tile_spec,
        out_shape=jax.ShapeDtypeStruct((bsz * seq, d), x.dtype),
        scratch_shapes=scratch,
        compiler_params=pltpu.CompilerParams(
            dimension_semantics=("arbitrary",),
            vmem_limit_bytes=VMEM_LIMIT_BYTES),
    )(x.reshape(bsz * seq, d), *operands)
    return out.reshape(bsz, seq, d)


def _row(v):
    return v.reshape(1, -1).astype(F32)


def _ab_layer(x, i, norm_g, w_in, pool_w, pool_b, pool_scale, conv_w, conv_b, w_out):
    d = x.shape[-1]
    d_in, d_mid = w_in.shape[2], w_out.shape[1]
    assert pool_b.shape[0] == conv_b.shape[0] == W_COLS and d_in == 4 * W_COLS
    assert d_mid % W_ROWS == 0 and w_out.shape[2] == d
    n_in, n_out = d_in // W_COLS, d_mid // W_ROWS
    n_load = max(n_in, n_out)
    consts = [_row(norm_g), pool_w.astype(BF16), _row(pool_b), _row(pool_scale),
              conv_w.astype(F32), _row(conv_b)]
    operands = [consts[0], w_in] + consts[1:] + [w_out]
    in_specs = ([_const_spec(consts[0].shape), _col_chunk_spec(w_in, i, 0, n_in)]
                + [_const_spec(c.shape) for c in consts[1:]] + [_row_chunk_spec(w_out, i, n_out)])
    scratch = [pltpu.VMEM((n_in, d, W_COLS), BF16),
               pltpu.VMEM((d_mid, d), BF16),
               pltpu.VMEM((POOL_HIST, W_COLS), F32), pltpu.VMEM((CONV_HIST, W_COLS), F32)]
    return _call(_ab_kernel, x, operands, in_specs, scratch, AB_TILE, n_load, "pool_conv_mixer")


def _gla_layer(x, i, norm_g, w_in, w_g2, b_g, head_norm, w_out):
    d = x.shape[-1]
    dk = w_g2.shape[1]
    dv = w_out.shape[1]
    hk = dk // GLA_HEADS
    hv = dv // GLA_HEADS
    assert dk == W_COLS and dv == 2 * W_COLS and hv % hk == 0 and dv % W_ROWS == 0
    n_in, n_out = (2 * dk + 2 * dv) // W_COLS, dv // W_ROWS
    n_load = max(n_in, n_out)
    wgl = jnp.pad(w_in[i, :, 2 * dk + 2 * dv:], ((0, 0), (0, LANES - GLA_RANK)))
    wg2 = jnp.pad(w_g2, ((0, LANES - GLA_RANK), (0, 0)))
    consts = [wgl.astype(BF16), wg2.astype(BF16), _row(b_g), _row(jnp.tile(head_norm, GLA_HEADS))]
    operands = [_row(norm_g), w_in] + consts + [w_out]
    in_specs = ([_const_spec((1, d)), _col_chunk_spec(w_in, i, 0, n_in)]
                + [_const_spec(c.shape) for c in consts] + [_row_chunk_spec(w_out, i, n_out)])
    tm = GLA_TILE
    scratch = [pltpu.VMEM((n_in, d, W_COLS), BF16),
               pltpu.VMEM((dv, d), BF16),
               pltpu.VMEM((GLA_HEADS, hk, hv), F32),
               pltpu.VMEM((tm, d), BF16),
               pltpu.VMEM((tm, dk), F32),
               pltpu.VMEM((tm, dk), F32),
               pltpu.VMEM((tm, dv), BF16),
               pltpu.VMEM((tm, dk), F32),
               pltpu.VMEM((tm, dv), F32),
               pltpu.VMEM((tm, dv), BF16)]
    body = functools.partial(_gla_kernel, hk=hk, hv=hv)
    return _call(body, x, operands, in_specs, scratch, tm, n_load, "gla_mixer")


def _ffn_layer(x, l, norm_g, w_up, conv_w, conv_b, w_down, final_g, final_norm):
    d = x.shape[-1]
    d_ff = w_down.shape[1]
    assert d_ff % MXU_COLS == 0 and (2 * d_ff) % W_COLS == 0 and d_ff % W_ROWS == 0
    n_in, n_out = 2 * d_ff // W_COLS, d_ff // W_ROWS
    n_load = max(n_in, n_out)
    consts = [conv_w.astype(F32), _row(conv_b)]
    operands = [_row(norm_g), w_up] + consts + [w_down, _row(final_g)]
    in_specs = ([_const_spec((1, d)), _col_chunk_spec(w_up, l, 0, n_in)]
                + [_const_spec(c.shape) for c in consts]
                + [_row_chunk_spec(w_down, l, n_out), _const_spec((1, d))])
    tm = FFN_TILE
    scratch = [pltpu.VMEM((n_in, d, W_COLS), BF16),
               pltpu.VMEM((d_ff, d), BF16),
               pltpu.VMEM((CONV_HIST, d_ff), F32),
               pltpu.VMEM((tm, d), BF16),
               pltpu.VMEM((tm, d_ff), BF16)]
    body = functools.partial(_ffn_kernel, d_ff=d_ff, final_norm=final_norm)
    return _call(body, x, operands, in_specs, scratch, tm, n_load, "convffn")


def kernel(x, mix_norm, ffn_norm, ab_w_in, pool_w, pool_b, pool_scale, sc_conv_w, sc_conv_b,
           ab_w_out, gla_w_in, gla_w_g2, gla_b_g, gla_norm, gla_w_out, ffn_w_up, ffn_conv_w,
           ffn_conv_b, ffn_w_down, final_norm):
    depth = mix_norm.shape[0]
    seq = x.shape[1]
    assert seq % AB_TILE == 0 and seq % GLA_TILE == 0 and seq % FFN_TILE == 0 and GLA_TILE % CHUNK == 0
    for l in range(depth):
        i = l // 2
        if l % 2 == 0:
            x = _ab_layer(x, i, mix_norm[l], ab_w_in, pool_w[i], pool_b[i], pool_scale[i],
                          sc_conv_w[i], sc_conv_b[i], ab_w_out)
        else:
            x = _gla_layer(x, i, mix_norm[l], gla_w_in, gla_w_g2[i], gla_b_g[i], gla_norm[i],
                           gla_w_out)
        x = _ffn_layer(x, l, ffn_norm[l], ffn_w_up, ffn_conv_w[l], ffn_conv_b[l], ffn_w_down,
                       final_norm, final_norm=(l == depth - 1))
    return x
```

```python
import functools
import math

import jax
import jax.numpy as jnp
from jax import lax
from jax.experimental import pallas as pl
from jax.experimental.pallas import tpu as pltpu

F32 = jnp.float32
BF16 = jnp.bfloat16

EPS = 1e-6
CHUNK = 64
POOL_WINDOWS = (2, 4, 8, 16)
POOL_GROUP = 128
CONV_WIDTH = 3
GLA_HEADS = 4
GLA_RANK = 16
GLA_TAU = 16.0

LANES = 128
SUBLANES = 8
MXU_COLS = 256
POOL_HIST = 16
CONV_HIST = SUBLANES
FF_CHUNK = 4 * MXU_COLS
W_COLS = 2 * MXU_COLS
W_ROWS = MXU_COLS
AB_TILE = 2048
AB_SUB = 512
GLA_TILE = 1024
GLA_SUB = 256
GLA_OUT_CHUNKS = 8
FFN_TILE = 1024
VMEM_LIMIT_BYTES = 56 * 1024 * 1024

SQRT_HALF = math.sqrt(0.5)


def _rms(x, g):
    ms = jnp.mean(x * x, axis=-1, keepdims=True)
    return x * lax.rsqrt(ms + EPS) * g


def _shifted(hist, cur, shift):
    ext = jnp.concatenate([hist, cur], axis=0)
    return pltpu.roll(ext, shift, axis=0)[hist.shape[0]:]


def _causal_conv3(hist, u, w, b):
    u1 = _shifted(hist, u, 1)
    u2 = _shifted(hist, u, 2)
    return b + w[0:1] * u2 + w[1:2] * u1 + w[2:3] * u


def _dot(a, b):
    return jnp.dot(a, b, preferred_element_type=F32)


def _load_weights(t, win_ref, win_s, wout_ref, wout_s):
    @pl.when(t < win_s.shape[0])
    def _():
        win_s[t] = win_ref[...].astype(BF16)

    rows = wout_ref.shape[0]

    @pl.when(t < wout_s.shape[0] // rows)
    def _():
        wout_s[pl.ds(pl.multiple_of(t * rows, rows), rows), :] = wout_ref[...].astype(BF16)


def _ab_kernel(x_ref, g_ref, win_ref, pw_ref, pb_ref, ps_ref, cw_ref, cb_ref, wout_ref,
               o_ref, win_s, wout_s, pool_hist, conv_hist, *, tm, n_load, tiles_per_seq):
    t = pl.program_id(0)

    @pl.when(t < n_load)
    def _():
        _load_weights(t, win_ref, win_s, wout_ref, wout_s)

    @pl.when(t >= n_load)
    def _():
        s = lax.rem(t - n_load, tiles_per_seq)

        @pl.when(s == 0)
        def _():
            pool_hist[...] = jnp.zeros_like(pool_hist)
            conv_hist[...] = jnp.zeros_like(conv_hist)

        n_sub = tm // AB_SUB
        proj = []
        for j in range(n_sub):
            hn = _rms(x_ref[j * AB_SUB:(j + 1) * AB_SUB, :], g_ref[...]).astype(BF16)
            proj.append([_dot(hn, win_s[k]) for k in range(4)])

        row = lax.broadcasted_iota(jnp.int32, (AB_SUB, POOL_GROUP), 0)
        pool_prev, conv_prev = pool_hist[...], conv_hist[...]
        for j in range(n_sub):
            rows = slice(j * AB_SUB, (j + 1) * AB_SUB)
            pu, sb, sc, sx = proj[j]

            frame = row + (s * tm + j * AB_SUB + 1)
            ext = jnp.concatenate([pool_prev, pu], axis=0)
            ya = []
            for gi, win in enumerate(POOL_WINDOWS):
                lo = gi * POOL_GROUP
                e = ext[:, lo:lo + POOL_GROUP]
                sh = 1
                while sh < win:
                    e = e + pltpu.roll(e, sh, axis=0)
                    sh *= 2
                ug = pu[:, lo:lo + POOL_GROUP]
                mean = e[POOL_HIST:] / jnp.minimum(frame, win).astype(F32)
                ya.append(_dot((mean - ug).astype(BF16), pw_ref[gi]))
            ya = (jnp.concatenate(ya, axis=-1) + pb_ref[...]) * ps_ref[...]

            z = sc * sx
            yb = sb * _causal_conv3(conv_prev, z, cw_ref[...], cb_ref[...])

            y = jnp.concatenate([ya, yb], axis=-1).astype(BF16)
            o_ref[rows, :] = x_ref[rows, :] + _dot(y, wout_s[...])
            pool_prev, conv_prev = pu[AB_SUB - POOL_HIST:], z[AB_SUB - CONV_HIST:]

        pool_hist[...] = pool_prev
        conv_hist[...] = conv_prev


def _chunk_cumsum(e):
    sh = 1
    while sh < CHUNK:
        e = e + jnp.concatenate([jnp.zeros((sh, e.shape[1]), F32), e[:CHUNK - sh]], axis=0)
        sh *= 2
    return e


def _zero_after(v):
    bits = lax.shift_right_logical(lax.shift_right_logical(pltpu.bitcast(v, jnp.int32), 16), 16)
    return bits.astype(F32)


def _gla_kernel(x_ref, g_ref, win_ref, wgl_ref, wg2_ref, bg_ref, ng_ref, wout_ref, nxt_up_ref,
                nxt_dn_ref, o_ref, nxt_up_o, nxt_dn_o, win_s, wout_s, state_ref, hn_s, q_s, k_s, v_s,
                gate_s, rg_s, y_s, *, tm, hk, hv, n_load, tiles_per_seq, n_nxt_up, n_nxt_dn):
    t = pl.program_id(0)

    @pl.when(t < n_load)
    def _():
        _load_weights(t, win_ref, win_s, wout_ref, wout_s)

    @pl.when(t >= n_load)
    def _():
        @pl.when(t - n_load < n_nxt_up)
        def _():
            nxt_up_o[...] = nxt_up_ref[...].astype(BF16)

        @pl.when(t - n_load < n_nxt_dn)
        def _():
            nxt_dn_o[...] = nxt_dn_ref[...].astype(BF16)

        @pl.when(lax.rem(t - n_load, tiles_per_seq) == 0)
        def _():
            state_ref[...] = jnp.zeros_like(state_ref)

        anchors = []
        for i in range(tm // GLA_SUB):
            rs = slice(i * GLA_SUB, (i + 1) * GLA_SUB)
            hn_s[rs, :] = _rms(x_ref[rs, :], g_ref[...]).astype(BF16)
            gl = _dot(hn_s[rs, :], wgl_ref[...])
            gpre = _dot(gl.astype(BF16), wg2_ref[...]) + bg_ref[...]
            q_s[rs, :] = _dot(hn_s[rs, :], win_s[0]) * (hk ** -0.5)
            k = _dot(hn_s[rs, :], win_s[1])
            k_s[rs, :] = k
            gate_s[rs, :] = ((jnp.minimum(gpre, 0.0) - jnp.log(1.0 + jnp.exp(-jnp.abs(gpre))))
                             * (1.0 / GLA_TAU))
            sub_anchors = [k[0:1, :]]
            for j in range(2):
                cols = slice(j * W_COLS, (j + 1) * W_COLS)
                v = _dot(hn_s[rs, :], win_s[2 + j])
                v_s[rs, cols] = v.astype(BF16)
                sub_anchors.append(v[0:1, :])
            for j in range(2):
                cols = slice(j * W_COLS, (j + 1) * W_COLS)
                r = _dot(hn_s[rs, :], win_s[4 + j])
                rg_s[rs, cols] = r * jax.nn.sigmoid(r) * ng_ref[:, cols]
                if j == 0:
                    sub_anchors.append(r[0:1, :])
            anchors.append(sub_anchors)

        ri = lax.broadcasted_iota(jnp.int32, (CHUNK, CHUNK), 0)
        ci = lax.broadcasted_iota(jnp.int32, (CHUNK, CHUNK), 1)
        causal = ri >= ci

        n_chunks = tm // CHUNK
        nt = (((1,), (1,)), ((), ()))
        tn = (((0,), (0,)), ((), ()))
        heads = [(slice(hd * hk, (hd + 1) * hk), slice(hd * hv, (hd + 1) * hv))
                 for hd in range(GLA_HEADS)]

        def chunk_prep(c):
            rows = slice(c * CHUNK, (c + 1) * CHUNK)
            sub, c_sub = divmod(c, GLA_SUB // CHUNK)
            anchor = anchors[sub][c_sub * len(anchors[sub]) // (GLA_SUB // CHUNK)]
            gates = gate_s[rows, :] + _zero_after(anchor)
            bcum = _chunk_cumsum(gates)
            blast = bcum[CHUNK - 1:CHUNK]
            kc = k_s[rows, :]
            q_in = (q_s[rows, :] * jnp.exp(bcum)).astype(BF16)
            k_in = (kc * jnp.exp(-bcum)).astype(BF16)
            k_out = (kc * jnp.exp(blast - bcum)).astype(BF16)
            decay = jnp.exp(blast)
            atts, dcols = [], []
            for ks, _ in heads:
                att = lax.dot_general(q_in[:, ks], k_in[:, ks], nt, preferred_element_type=F32)
                atts.append(jnp.where(causal, att, 0.0).astype(BF16))
                dcols.append(jnp.broadcast_to(decay[:, ks], (hk, hk)).T)
            return rows, q_in, k_out, dcols, atts

        states = [state_ref[hd] for hd in range(GLA_HEADS)]
        prep = chunk_prep(0)
        for c in range(n_chunks):
            nxt = chunk_prep(c + 1) if c + 1 < n_chunks else None
            rows, q_in, k_out, dcols, atts = prep
            kv = [lax.dot_general(k_out[:, ks], v_s[rows, vs], tn, preferred_element_type=F32)
                  for ks, vs in heads]
            for hd, (ks, vs) in enumerate(heads):
                lhs = jnp.concatenate([q_in[:, ks], atts[hd]], axis=1)
                rhs = jnp.concatenate([states[hd].astype(BF16), v_s[rows, vs]], axis=0)
                o = _dot(lhs, rhs)
                states[hd] = states[hd] * jnp.concatenate([dcols[hd]] * (hv // hk), axis=1) + kv[hd]
                o = o * lax.rsqrt(jnp.mean(o * o, axis=-1, keepdims=True) + EPS)
                y_s[rows, vs] = (o * rg_s[rows, vs]).astype(BF16)
            if (c + 1) % GLA_OUT_CHUNKS == 0:
                done = slice((c + 1 - GLA_OUT_CHUNKS) * CHUNK, (c + 1) * CHUNK)
                o_ref[done, :] = x_ref[done, :] + _dot(y_s[done, :], wout_s[...])
            prep = nxt

        for hd in range(GLA_HEADS):
            state_ref[hd] = states[hd]


def _ffn_kernel(x_ref, g_ref, wup_ref, cw_ref, cb_ref, wdn_ref, fg_ref, o_ref, *scratch,
                tm, d_ff, final_norm, n_load, tiles_per_seq):
    t = pl.program_id(0)
    if n_load:
        wup_s, wdn_s, conv_hist, hn_s, a_s = scratch

        @pl.when(t < n_load)
        def _():
            _load_weights(t, wup_ref, wup_s, wdn_ref, wdn_s)
    else:
        wup_s, wdn_s = wup_ref, wdn_ref
        conv_hist, hn_s, a_s = scratch

    @pl.when(t >= n_load)
    def _():
        @pl.when(lax.rem(t - n_load, tiles_per_seq) == 0)
        def _():
            conv_hist[...] = jnp.zeros_like(conv_hist)

        x = x_ref[...]
        hn_s[...] = _rms(x, g_ref[...]).astype(BF16)

        def up_proj(c0, width):
            pieces = []
            while width > 0:
                j, lo = divmod(c0, W_COLS)
                w = min(width, W_COLS - lo)
                pieces.append(_dot(hn_s[...], wup_s[j, :, lo:lo + w]))
                c0 += w
                width -= w
            return pieces[0] if len(pieces) == 1 else jnp.concatenate(pieces, axis=1)

        for c0 in range(0, d_ff, FF_CHUNK):
            cols = slice(c0, min(c0 + FF_CHUNK, d_ff))
            u = up_proj(c0, cols.stop - c0)
            v = up_proj(d_ff + c0, cols.stop - c0)
            uc = _causal_conv3(conv_hist[:, cols], u, cw_ref[:, cols], cb_ref[:, cols])
            a_s[:, cols] = (0.5 * uc * (1.0 + lax.erf(uc * SQRT_HALF)) * v).astype(BF16)
            conv_hist[:, cols] = u[tm - CONV_HIST:]

        out = x + _dot(a_s[...], wdn_s[...])
        if final_norm:
            out = _rms(out, fg_ref[...])
        o_ref[...] = out


def _const_spec(shape):
    nd = len(shape)
    return pl.BlockSpec(shape, lambda t: (0,) * nd, pipeline_mode=pl.Buffered(1))


def _col_chunk_spec(w, layer, lo, n):
    return pl.BlockSpec((None, w.shape[1], W_COLS), lambda t: (layer, 0, lo + jnp.minimum(t, n - 1)))


def _row_chunk_spec(w, layer, n):
    return pl.BlockSpec((None, W_ROWS, w.shape[2]), lambda t: (layer, jnp.minimum(t, n - 1), 0))


def _call(body, x, operands, in_specs, scratch, tm, n_load, name, extra_out_shapes=(), extra_out_specs=()):
    bsz, seq, d = x.shape
    n_tiles = bsz * seq // tm
    tile_spec = pl.BlockSpec((tm, d), lambda t: (jnp.maximum(t - n_load, 0), 0))
    outs = pl.pallas_call(
        functools.partial(body, tm=tm, n_load=n_load, tiles_per_seq=seq // tm),
        name=name,
        grid=(n_load + n_tiles,),
        in_specs=[tile_spec] + in_specs,
        out_specs=[tile_spec] + list(extra_out_specs),
        out_shape=[jax.ShapeDtypeStruct((bsz * seq, d), x.dtype)] + list(extra_out_shapes),
        scratch_shapes=scratch,
        compiler_params=pltpu.CompilerParams(
            dimension_semantics=("arbitrary",),
            vmem_limit_bytes=VMEM_LIMIT_BYTES),
    )(x.reshape(bsz * seq, d), *operands)
    return (outs[0].reshape(bsz, seq, d),) + tuple(outs[1:])


def _row(v):
    return v.reshape(1, -1).astype(F32)


def _ab_layer(x, i, norm_g, w_in, pool_w, pool_b, pool_scale, conv_w, conv_b, w_out):
    d = x.shape[-1]
    d_in, d_mid = w_in.shape[2], w_out.shape[1]
    assert pool_b.shape[0] == conv_b.shape[0] == W_COLS and d_in == 4 * W_COLS
    assert d_mid % W_ROWS == 0 and w_out.shape[2] == d
    n_in, n_out = d_in // W_COLS, d_mid // W_ROWS
    n_load = max(n_in, n_out)
    consts = [_row(norm_g), pool_w.astype(BF16), _row(pool_b), _row(pool_scale),
              conv_w.astype(F32), _row(conv_b)]
    operands = [consts[0], w_in] + consts[1:] + [w_out]
    in_specs = ([_const_spec(consts[0].shape), _col_chunk_spec(w_in, i, 0, n_in)]
                + [_const_spec(c.shape) for c in consts[1:]] + [_row_chunk_spec(w_out, i, n_out)])
    scratch = [pltpu.VMEM((n_in, d, W_COLS), BF16),
               pltpu.VMEM((d_mid, d), BF16),
               pltpu.VMEM((POOL_HIST, W_COLS), F32), pltpu.VMEM((CONV_HIST, W_COLS), F32)]
    return _call(_ab_kernel, x, operands, in_specs, scratch, AB_TILE, n_load, "pool_conv_mixer")[0]


def _gla_layer(x, i, norm_g, w_in, w_g2, b_g, head_norm, w_out, nxt_l, nxt_w_up, nxt_w_down):
    d = x.shape[-1]
    dk = w_g2.shape[1]
    dv = w_out.shape[1]
    hk = dk // GLA_HEADS
    hv = dv // GLA_HEADS
    assert dk == W_COLS and dv == 2 * W_COLS and hv % hk == 0 and dv % W_ROWS == 0
    n_in, n_out = (2 * dk + 2 * dv) // W_COLS, dv // W_ROWS
    n_load = max(n_in, n_out)
    wgl = jnp.pad(w_in[i, :, 2 * dk + 2 * dv:], ((0, 0), (0, LANES - GLA_RANK)))
    wg2 = jnp.pad(w_g2, ((0, LANES - GLA_RANK), (0, 0)))
    consts = [wgl.astype(BF16), wg2.astype(BF16), _row(b_g), _row(jnp.tile(head_norm, GLA_HEADS))]
    d_ff = nxt_w_down.shape[1]
    n_up, n_dn = nxt_w_up.shape[2] // W_COLS, d_ff // W_ROWS

    def during_tiles(n):
        return lambda t: jnp.clip(t - n_load, 0, n - 1)

    up_idx, dn_idx = during_tiles(n_up), during_tiles(n_dn)
    operands = [_row(norm_g), w_in] + consts + [w_out, nxt_w_up, nxt_w_down]
    in_specs = ([_const_spec((1, d)), _col_chunk_spec(w_in, i, 0, n_in)]
                + [_const_spec(c.shape) for c in consts] + [_row_chunk_spec(w_out, i, n_out)]
                + [pl.BlockSpec((None, d, W_COLS), lambda t: (nxt_l, 0, up_idx(t))),
                   pl.BlockSpec((None, W_ROWS, d), lambda t: (nxt_l, dn_idx(t), 0))])
    extra_out_shapes = [jax.ShapeDtypeStruct((n_up, d, W_COLS), BF16),
                        jax.ShapeDtypeStruct((d_ff, d), BF16)]
    extra_out_specs = [pl.BlockSpec((None, d, W_COLS), lambda t: (up_idx(t), 0, 0)),
                       pl.BlockSpec((W_ROWS, d), lambda t: (dn_idx(t), 0))]
    tm = GLA_TILE
    assert x.shape[0] * x.shape[1] // tm >= max(n_up, n_dn)
    scratch = [pltpu.VMEM((n_in, d, W_COLS), BF16),
               pltpu.VMEM((dv, d), BF16),
               pltpu.VMEM((GLA_HEADS, hk, hv), F32),
               pltpu.VMEM((tm, d), BF16),
               pltpu.VMEM((tm, dk), F32),
               pltpu.VMEM((tm, dk), F32),
               pltpu.VMEM((tm, dv), BF16),
               pltpu.VMEM((tm, dk), F32),
               pltpu.VMEM((tm, dv), F32),
               pltpu.VMEM((tm, dv), BF16)]
    body = functools.partial(_gla_kernel, hk=hk, hv=hv, n_nxt_up=n_up, n_nxt_dn=n_dn)
    return _call(body, x, operands, in_specs, scratch, tm, n_load, "gla_mixer",
                 extra_out_shapes, extra_out_specs)


def _ffn_layer(x, l, norm_g, w_up, conv_w, conv_b, w_down, final_g, final_norm, precast=None):
    d = x.shape[-1]
    d_ff = w_down.shape[1]
    assert d_ff % MXU_COLS == 0 and (2 * d_ff) % W_COLS == 0 and d_ff % W_ROWS == 0
    n_in, n_out = 2 * d_ff // W_COLS, d_ff // W_ROWS
    consts = [conv_w.astype(F32), _row(conv_b)]
    tm = FFN_TILE
    scratch = [pltpu.VMEM((CONV_HIST, d_ff), F32),
               pltpu.VMEM((tm, d), BF16),
               pltpu.VMEM((tm, d_ff), BF16)]
    if precast is None:
        n_load = max(n_in, n_out)
        w_up_op, w_down_op = w_up, w_down
        w_specs = [_col_chunk_spec(w_up, l, 0, n_in), _row_chunk_spec(w_down, l, n_out)]
        scratch = [pltpu.VMEM((n_in, d, W_COLS), BF16),
                   pltpu.VMEM((d_ff, d), BF16)] + scratch
    else:
        n_load = 0
        w_up_op, w_down_op = precast
        w_specs = [_const_spec(w_up_op.shape), _const_spec(w_down_op.shape)]
    operands = [_row(norm_g), w_up_op] + consts + [w_down_op, _row(final_g)]
    in_specs = ([_const_spec((1, d)), w_specs[0]] + [_const_spec(c.shape) for c in consts]
                + [w_specs[1], _const_spec((1, d))])
    body = functools.partial(_ffn_kernel, d_ff=d_ff, final_norm=final_norm)
    return _call(body, x, operands, in_specs, scratch, tm, n_load, "convffn")[0]


def kernel(x, mix_norm, ffn_norm, ab_w_in, pool_w, pool_b, pool_scale, sc_conv_w, sc_conv_b,
           ab_w_out, gla_w_in, gla_w_g2, gla_b_g, gla_norm, gla_w_out, ffn_w_up, ffn_conv_w,
           ffn_conv_b, ffn_w_down, final_norm):
    depth = mix_norm.shape[0]
    seq = x.shape[1]
    assert seq % AB_TILE == 0 and seq % GLA_TILE == 0 and seq % FFN_TILE == 0 and GLA_TILE % CHUNK == 0
    for l in range(depth):
        i = l // 2
        precast = None
        if l % 2 == 0:
            x = _ab_layer(x, i, mix_norm[l], ab_w_in, pool_w[i], pool_b[i], pool_scale[i],
                          sc_conv_w[i], sc_conv_b[i], ab_w_out)
        else:
            x, *precast = _gla_layer(x, i, mix_norm[l], gla_w_in, gla_w_g2[i], gla_b_g[i], gla_norm[i],
                                     gla_w_out, l, ffn_w_up, ffn_w_down)
        x = _ffn_layer(x, l, ffn_norm[l], ffn_w_up, ffn_conv_w[l], ffn_conv_b[l], ffn_w_down,
                       final_norm, final_norm=(l == depth - 1), precast=precast)
    return x
```

```python
import functools
import math

import jax
import jax.numpy as jnp
from jax import lax
from jax.experimental import pallas as pl
from jax.experimental.pallas import tpu as pltpu

F32 = jnp.float32
BF16 = jnp.bfloat16

EPS = 1e-6
CHUNK = 64
POOL_WINDOWS = (2, 4, 8, 16)
POOL_GROUP = 128
CONV_WIDTH = 3
GLA_HEADS = 4
GLA_RANK = 16
GLA_TAU = 16.0

LANES = 128
SUBLANES = 8
MXU_COLS = 256
POOL_HIST = 16
CONV_HIST = SUBLANES
FF_CHUNK = 4 * MXU_COLS
W_COLS = 2 * MXU_COLS
W_ROWS = MXU_COLS
AB_TILE = 2048
AB_SUB = 512
GLA_TILE = 1024
GLA_SUB = 256
GLA_OUT_CHUNKS = 8
FFN_TILE = 1024
VMEM_LIMIT_BYTES = 61 * 1024 * 1024

SQRT_HALF = math.sqrt(0.5)


def _rms(x, g):
    ms = jnp.mean(x * x, axis=-1, keepdims=True)
    return x * lax.rsqrt(ms + EPS) * g


def _shifted(hist, cur, shift):
    ext = jnp.concatenate([hist, cur], axis=0)
    return pltpu.roll(ext, shift, axis=0)[hist.shape[0]:]


def _causal_conv3(hist, u, w, b):
    u1 = _shifted(hist, u, 1)
    u2 = _shifted(hist, u, 2)
    return b + w[0:1] * u2 + w[1:2] * u1 + w[2:3] * u


def _dot(a, b):
    return jnp.dot(a, b, preferred_element_type=F32)


def _load_weights(t, win_ref, win_s, wout_ref, wout_s):
    @pl.when(t < win_s.shape[0])
    def _():
        win_s[t] = win_ref[...].astype(BF16)

    rows = wout_ref.shape[0]

    @pl.when(t < wout_s.shape[0] // rows)
    def _():
        wout_s[pl.ds(pl.multiple_of(t * rows, rows), rows), :] = wout_ref[...].astype(BF16)


def _cast_next_weights(k, nxt_up_ref, nxt_dn_ref, nxt_up_o, nxt_dn_o, n_up, n_dn):
    @pl.when(k < n_up)
    def _():
        nxt_up_o[...] = nxt_up_ref[...].astype(BF16)

    @pl.when(k < n_dn)
    def _():
        nxt_dn_o[...] = nxt_dn_ref[...].astype(BF16)


def _ab_kernel(x_ref, g_ref, win_ref, pw_ref, pb_ref, ps_ref, cw_ref, cb_ref, wout_ref, nxt_up_ref,
               nxt_dn_ref, o_ref, nxt_up_o, nxt_dn_o, win_s, wout_s, pool_hist, conv_hist,
               *, tm, n_load, tiles_per_seq, n_nxt_up, n_nxt_dn):
    t = pl.program_id(0)

    @pl.when(t < n_load)
    def _():
        _load_weights(t, win_ref, win_s, wout_ref, wout_s)

    @pl.when(t >= n_load)
    def _():
        _cast_next_weights(t - n_load, nxt_up_ref, nxt_dn_ref, nxt_up_o, nxt_dn_o, n_nxt_up, n_nxt_dn)
        s = lax.rem(t - n_load, tiles_per_seq)

        @pl.when(s == 0)
        def _():
            pool_hist[...] = jnp.zeros_like(pool_hist)
            conv_hist[...] = jnp.zeros_like(conv_hist)

        n_sub = tm // AB_SUB
        proj = []
        for j in range(n_sub):
            hn = _rms(x_ref[j * AB_SUB:(j + 1) * AB_SUB, :], g_ref[...]).astype(BF16)
            proj.append([_dot(hn, win_s[k]) for k in range(4)])

        row = lax.broadcasted_iota(jnp.int32, (AB_SUB, POOL_GROUP), 0)
        pool_prev, conv_prev = pool_hist[...], conv_hist[...]
        for j in range(n_sub):
            rows = slice(j * AB_SUB, (j + 1) * AB_SUB)
            pu, sb, sc, sx = proj[j]

            frame = row + (s * tm + j * AB_SUB + 1)
            ext = jnp.concatenate([pool_prev, pu], axis=0)
            ya = []
            for gi, win in enumerate(POOL_WINDOWS):
                lo = gi * POOL_GROUP
                e = ext[:, lo:lo + POOL_GROUP]
                sh = 1
                while sh < win:
                    e = e + pltpu.roll(e, sh, axis=0)
                    sh *= 2
                ug = pu[:, lo:lo + POOL_GROUP]
                mean = e[POOL_HIST:] / jnp.minimum(frame, win).astype(F32)
                ya.append(_dot((mean - ug).astype(BF16), pw_ref[gi]))
            ya = (jnp.concatenate(ya, axis=-1) + pb_ref[...]) * ps_ref[...]

            z = sc * sx
            yb = sb * _causal_conv3(conv_prev, z, cw_ref[...], cb_ref[...])

            y = jnp.concatenate([ya, yb], axis=-1).astype(BF16)
            o_ref[rows, :] = x_ref[rows, :] + _dot(y, wout_s[...])
            pool_prev, conv_prev = pu[AB_SUB - POOL_HIST:], z[AB_SUB - CONV_HIST:]

        pool_hist[...] = pool_prev
        conv_hist[...] = conv_prev


def _chunk_cumsum(e):
    sh = 1
    while sh < CHUNK:
        e = e + jnp.concatenate([jnp.zeros((sh, e.shape[1]), F32), e[:CHUNK - sh]], axis=0)
        sh *= 2
    return e


def _zero_after(v):
    bits = lax.shift_right_logical(lax.shift_right_logical(pltpu.bitcast(v, jnp.int32), 16), 16)
    return bits.astype(F32)


def _gla_kernel(x_ref, g_ref, win_ref, wgl_ref, wg2_ref, bg_ref, ng_ref, wout_ref, nxt_up_ref,
                nxt_dn_ref, o_ref, nxt_up_o, nxt_dn_o, win_s, wout_s, state_ref, hn_s, q_s, k_s, v_s,
                gate_s, rg_s, y_s, *, tm, hk, hv, n_load, tiles_per_seq, n_nxt_up, n_nxt_dn):
    t = pl.program_id(0)

    @pl.when(t < n_load)
    def _():
        _load_weights(t, win_ref, win_s, wout_ref, wout_s)

    @pl.when(t >= n_load)
    def _():
        _cast_next_weights(t - n_load, nxt_up_ref, nxt_dn_ref, nxt_up_o, nxt_dn_o, n_nxt_up, n_nxt_dn)

        @pl.when(lax.rem(t - n_load, tiles_per_seq) == 0)
        def _():
            state_ref[...] = jnp.zeros_like(state_ref)

        anchors = []
        for i in range(tm // GLA_SUB):
            rs = slice(i * GLA_SUB, (i + 1) * GLA_SUB)
            hn_s[rs, :] = _rms(x_ref[rs, :], g_ref[...]).astype(BF16)
            gl = _dot(hn_s[rs, :], wgl_ref[...])
            gpre = _dot(gl.astype(BF16), wg2_ref[...]) + bg_ref[...]
            q_s[rs, :] = _dot(hn_s[rs, :], win_s[0]) * (hk ** -0.5)
            k = _dot(hn_s[rs, :], win_s[1])
            k_s[rs, :] = k
            gate_s[rs, :] = ((jnp.minimum(gpre, 0.0) - jnp.log(1.0 + jnp.exp(-jnp.abs(gpre))))
                             * (1.0 / GLA_TAU))
            sub_anchors = [k[0:1, :]]
            for j in range(2):
                cols = slice(j * W_COLS, (j + 1) * W_COLS)
                v = _dot(hn_s[rs, :], win_s[2 + j])
                v_s[rs, cols] = v.astype(BF16)
                sub_anchors.append(v[0:1, :])
            for j in range(2):
                cols = slice(j * W_COLS, (j + 1) * W_COLS)
                r = _dot(hn_s[rs, :], win_s[4 + j])
                rg_s[rs, cols] = r * jax.nn.sigmoid(r) * ng_ref[:, cols]
                if j == 0:
                    sub_anchors.append(r[0:1, :])
            anchors.append(sub_anchors)

        ri = lax.broadcasted_iota(jnp.int32, (CHUNK, CHUNK), 0)
        ci = lax.broadcasted_iota(jnp.int32, (CHUNK, CHUNK), 1)
        causal = ri >= ci

        n_chunks = tm // CHUNK
        nt = (((1,), (1,)), ((), ()))
        tn = (((0,), (0,)), ((), ()))
        heads = [(slice(hd * hk, (hd + 1) * hk), slice(hd * hv, (hd + 1) * hv))
                 for hd in range(GLA_HEADS)]

        def chunk_prep(c):
            rows = slice(c * CHUNK, (c + 1) * CHUNK)
            sub, c_sub = divmod(c, GLA_SUB // CHUNK)
            anchor = anchors[sub][c_sub * len(anchors[sub]) // (GLA_SUB // CHUNK)]
            gates = gate_s[rows, :] + _zero_after(anchor)
            bcum = _chunk_cumsum(gates)
            blast = bcum[CHUNK - 1:CHUNK]
            kc = k_s[rows, :]
            q_in = (q_s[rows, :] * jnp.exp(bcum)).astype(BF16)
            k_in = (kc * jnp.exp(-bcum)).astype(BF16)
            k_out = (kc * jnp.exp(blast - bcum)).astype(BF16)
            decay = jnp.exp(blast)
            atts, dcols = [], []
            for ks, _ in heads:
                att = lax.dot_general(q_in[:, ks], k_in[:, ks], nt, preferred_element_type=F32)
                atts.append(jnp.where(causal, att, 0.0).astype(BF16))
                dcols.append(jnp.broadcast_to(decay[:, ks], (hk, hk)).T)
            return rows, q_in, k_out, dcols, atts

        states = [state_ref[hd] for hd in range(GLA_HEADS)]
        prep = chunk_prep(0)
        for c in range(n_chunks):
            nxt = chunk_prep(c + 1) if c + 1 < n_chunks else None
            rows, q_in, k_out, dcols, atts = prep
            kv = [lax.dot_general(k_out[:, ks], v_s[rows, vs], tn, preferred_element_type=F32)
                  for ks, vs in heads]
            for hd, (ks, vs) in enumerate(heads):
                lhs = jnp.concatenate([q_in[:, ks], atts[hd]], axis=1)
                rhs = jnp.concatenate([states[hd].astype(BF16), v_s[rows, vs]], axis=0)
                o = _dot(lhs, rhs)
                states[hd] = states[hd] * jnp.concatenate([dcols[hd]] * (hv // hk), axis=1) + kv[hd]
                o = o * lax.rsqrt(jnp.mean(o * o, axis=-1, keepdims=True) + EPS)
                y_s[rows, vs] = (o * rg_s[rows, vs]).astype(BF16)
            if (c + 1) % GLA_OUT_CHUNKS == 0:
                done = slice((c + 1 - GLA_OUT_CHUNKS) * CHUNK, (c + 1) * CHUNK)
                o_ref[done, :] = x_ref[done, :] + _dot(y_s[done, :], wout_s[...])
            prep = nxt

        for hd in range(GLA_HEADS):
            state_ref[hd] = states[hd]


def _ffn_kernel(x_ref, g_ref, wup_ref, cw_ref, cb_ref, wdn_ref, fg_ref, o_ref, *scratch,
                tm, d_ff, final_norm, n_load, tiles_per_seq):
    t = pl.program_id(0)
    if n_load:
        wup_s, wdn_s, conv_hist, hn_s, a_s = scratch

        @pl.when(t < n_load)
        def _():
            _load_weights(t, wup_ref, wup_s, wdn_ref, wdn_s)
    else:
        wup_s, wdn_s = wup_ref, wdn_ref
        conv_hist, hn_s, a_s = scratch

    @pl.when(t >= n_load)
    def _():
        @pl.when(lax.rem(t - n_load, tiles_per_seq) == 0)
        def _():
            conv_hist[...] = jnp.zeros_like(conv_hist)

        x = x_ref[...]
        hn_s[...] = _rms(x, g_ref[...]).astype(BF16)

        def up_proj(c0, width):
            pieces = []
            while width > 0:
                j, lo = divmod(c0, W_COLS)
                w = min(width, W_COLS - lo)
                pieces.append(_dot(hn_s[...], wup_s[j, :, lo:lo + w]))
                c0 += w
                width -= w
            return pieces[0] if len(pieces) == 1 else jnp.concatenate(pieces, axis=1)

        for c0 in range(0, d_ff, FF_CHUNK):
            cols = slice(c0, min(c0 + FF_CHUNK, d_ff))
            u = up_proj(c0, cols.stop - c0)
            v = up_proj(d_ff + c0, cols.stop - c0)
            uc = _causal_conv3(conv_hist[:, cols], u, cw_ref[:, cols], cb_ref[:, cols])
            a_s[:, cols] = (0.5 * uc * (1.0 + lax.erf(uc * SQRT_HALF)) * v).astype(BF16)
            conv_hist[:, cols] = u[tm - CONV_HIST:]

        out = x + _dot(a_s[...], wdn_s[...])
        if final_norm:
            out = _rms(out, fg_ref[...])
        o_ref[...] = out


def _const_spec(shape):
    nd = len(shape)
    return pl.BlockSpec(shape, lambda t: (0,) * nd, pipeline_mode=pl.Buffered(1))


def _col_chunk_spec(w, layer, lo, n):
    return pl.BlockSpec((None, w.shape[1], W_COLS), lambda t: (layer, 0, lo + jnp.minimum(t, n - 1)))


def _row_chunk_spec(w, layer, n):
    return pl.BlockSpec((None, W_ROWS, w.shape[2]), lambda t: (layer, jnp.minimum(t, n - 1), 0))


def _call(body, x, operands, in_specs, scratch, tm, n_load, name, extra_out_shapes=(), extra_out_specs=()):
    bsz, seq, d = x.shape
    n_tiles = bsz * seq // tm
    tile_spec = pl.BlockSpec((tm, d), lambda t: (jnp.maximum(t - n_load, 0), 0))
    outs = pl.pallas_call(
        functools.partial(body, tm=tm, n_load=n_load, tiles_per_seq=seq // tm),
        name=name,
        grid=(n_load + n_tiles,),
        in_specs=[tile_spec] + in_specs,
        out_specs=[tile_spec] + list(extra_out_specs),
        out_shape=[jax.ShapeDtypeStruct((bsz * seq, d), x.dtype)] + list(extra_out_shapes),
        scratch_shapes=scratch,
        compiler_params=pltpu.CompilerParams(
            dimension_semantics=("arbitrary",),
            vmem_limit_bytes=VMEM_LIMIT_BYTES),
    )(x.reshape(bsz * seq, d), *operands)
    return (outs[0].reshape(bsz, seq, d),) + tuple(outs[1:])


def _row(v):
    return v.reshape(1, -1).astype(F32)


def _next_weight_io(nxt_l, nxt_w_up, nxt_w_down, n_load, n_tiles):
    d, d_ff = nxt_w_up.shape[1], nxt_w_down.shape[1]
    n_up, n_dn = nxt_w_up.shape[2] // W_COLS, d_ff // W_ROWS
    assert n_tiles >= max(n_up, n_dn)

    def up_idx(t):
        return jnp.clip(t - n_load, 0, n_up - 1)

    def dn_idx(t):
        return jnp.clip(t - n_load, 0, n_dn - 1)

    in_specs = [pl.BlockSpec((None, d, W_COLS), lambda t: (nxt_l, 0, up_idx(t))),
                pl.BlockSpec((None, W_ROWS, d), lambda t: (nxt_l, dn_idx(t), 0))]
    out_shapes = [jax.ShapeDtypeStruct((n_up, d, W_COLS), BF16), jax.ShapeDtypeStruct((d_ff, d), BF16)]
    out_specs = [pl.BlockSpec((None, d, W_COLS), lambda t: (up_idx(t), 0, 0)),
                 pl.BlockSpec((W_ROWS, d), lambda t: (dn_idx(t), 0))]
    return in_specs, out_shapes, out_specs, n_up, n_dn


def _ab_layer(x, i, norm_g, w_in, pool_w, pool_b, pool_scale, conv_w, conv_b, w_out,
              nxt_l, nxt_w_up, nxt_w_down):
    d = x.shape[-1]
    d_in, d_mid = w_in.shape[2], w_out.shape[1]
    assert pool_b.shape[0] == conv_b.shape[0] == W_COLS and d_in == 4 * W_COLS
    assert d_mid % W_ROWS == 0 and w_out.shape[2] == d
    n_in, n_out = d_in // W_COLS, d_mid // W_ROWS
    n_load = max(n_in, n_out)
    consts = [_row(norm_g), pool_w.astype(BF16), _row(pool_b), _row(pool_scale),
              conv_w.astype(F32), _row(conv_b)]
    nxt_in, nxt_shapes, nxt_out, n_up, n_dn = _next_weight_io(
        nxt_l, nxt_w_up, nxt_w_down, n_load, x.shape[0] * x.shape[1] // AB_TILE)
    operands = [consts[0], w_in] + consts[1:] + [w_out, nxt_w_up, nxt_w_down]
    in_specs = ([_const_spec(consts[0].shape), _col_chunk_spec(w_in, i, 0, n_in)]
                + [_const_spec(c.shape) for c in consts[1:]] + [_row_chunk_spec(w_out, i, n_out)] + nxt_in)
    scratch = [pltpu.VMEM((n_in, d, W_COLS), BF16),
               pltpu.VMEM((d_mid, d), BF16),
               pltpu.VMEM((POOL_HIST, W_COLS), F32), pltpu.VMEM((CONV_HIST, W_COLS), F32)]
    body = functools.partial(_ab_kernel, n_nxt_up=n_up, n_nxt_dn=n_dn)
    return _call(body, x, operands, in_specs, scratch, AB_TILE, n_load, "pool_conv_mixer",
                 nxt_shapes, nxt_out)


def _gla_layer(x, i, norm_g, w_in, w_g2, b_g, head_norm, w_out, nxt_l, nxt_w_up, nxt_w_down):
    d = x.shape[-1]
    dk = w_g2.shape[1]
    dv = w_out.shape[1]
    hk = dk // GLA_HEADS
    hv = dv // GLA_HEADS
    assert dk == W_COLS and dv == 2 * W_COLS and hv % hk == 0 and dv % W_ROWS == 0
    n_in, n_out = (2 * dk + 2 * dv) // W_COLS, dv // W_ROWS
    n_load = max(n_in, n_out)
    wgl = jnp.pad(w_in[i, :, 2 * dk + 2 * dv:], ((0, 0), (0, LANES - GLA_RANK)))
    wg2 = jnp.pad(w_g2, ((0, LANES - GLA_RANK), (0, 0)))
    consts = [wgl.astype(BF16), wg2.astype(BF16), _row(b_g), _row(jnp.tile(head_norm, GLA_HEADS))]
    tm = GLA_TILE
    nxt_in, extra_out_shapes, extra_out_specs, n_up, n_dn = _next_weight_io(
        nxt_l, nxt_w_up, nxt_w_down, n_load, x.shape[0] * x.shape[1] // tm)
    operands = [_row(norm_g), w_in] + consts + [w_out, nxt_w_up, nxt_w_down]
    in_specs = ([_const_spec((1, d)), _col_chunk_spec(w_in, i, 0, n_in)]
                + [_const_spec(c.shape) for c in consts] + [_row_chunk_spec(w_out, i, n_out)] + nxt_in)
    scratch = [pltpu.VMEM((n_in, d, W_COLS), BF16),
               pltpu.VMEM((dv, d), BF16),
               pltpu.VMEM((GLA_HEADS, hk, hv), F32),
               pltpu.VMEM((tm, d), BF16),
               pltpu.VMEM((tm, dk), F32),
               pltpu.VMEM((tm, dk), F32),
               pltpu.VMEM((tm, dv), BF16),
               pltpu.VMEM((tm, dk), F32),
               pltpu.VMEM((tm, dv), F32),
               pltpu.VMEM((tm, dv), BF16)]
    body = functools.partial(_gla_kernel, hk=hk, hv=hv, n_nxt_up=n_up, n_nxt_dn=n_dn)
    return _call(body, x, operands, in_specs, scratch, tm, n_load, "gla_mixer",
                 extra_out_shapes, extra_out_specs)


def _ffn_layer(x, l, norm_g, w_up, conv_w, conv_b, w_down, final_g, final_norm, precast=None):
    d = x.shape[-1]
    d_ff = w_down.shape[1]
    assert d_ff % MXU_COLS == 0 and (2 * d_ff) % W_COLS == 0 and d_ff % W_ROWS == 0
    n_in, n_out = 2 * d_ff // W_COLS, d_ff // W_ROWS
    consts = [conv_w.astype(F32), _row(conv_b)]
    tm = FFN_TILE
    scratch = [pltpu.VMEM((CONV_HIST, d_ff), F32),
               pltpu.VMEM((tm, d), BF16),
               pltpu.VMEM((tm, d_ff), BF16)]
    if precast is None:
        n_load = max(n_in, n_out)
        w_up_op, w_down_op = w_up, w_down
        w_specs = [_col_chunk_spec(w_up, l, 0, n_in), _row_chunk_spec(w_down, l, n_out)]
        scratch = [pltpu.VMEM((n_in, d, W_COLS), BF16),
                   pltpu.VMEM((d_ff, d), BF16)] + scratch
    else:
        n_load = 0
        w_up_op, w_down_op = precast
        w_specs = [_const_spec(w_up_op.shape), _const_spec(w_down_op.shape)]
    operands = [_row(norm_g), w_up_op] + consts + [w_down_op, _row(final_g)]
    in_specs = ([_const_spec((1, d)), w_specs[0]] + [_const_spec(c.shape) for c in consts]
                + [w_specs[1], _const_spec((1, d))])
    body = functools.partial(_ffn_kernel, d_ff=d_ff, final_norm=final_norm)
    return _call(body, x, operands, in_specs, scratch, tm, n_load, "convffn")[0]


def kernel(x, mix_norm, ffn_norm, ab_w_in, pool_w, pool_b, pool_scale, sc_conv_w, sc_conv_b,
           ab_w_out, gla_w_in, gla_w_g2, gla_b_g, gla_norm, gla_w_out, ffn_w_up, ffn_conv_w,
           ffn_conv_b, ffn_w_down, final_norm):
    depth = mix_norm.shape[0]
    seq = x.shape[1]
    assert seq % AB_TILE == 0 and seq % GLA_TILE == 0 and seq % FFN_TILE == 0 and GLA_TILE % CHUNK == 0
    for l in range(depth):
        i = l // 2
        if l % 2 == 0:
            x, *precast = _ab_layer(x, i, mix_norm[l], ab_w_in, pool_w[i], pool_b[i], pool_scale[i],
                                    sc_conv_w[i], sc_conv_b[i], ab_w_out, l, ffn_w_up, ffn_w_down)
        else:
            x, *precast = _gla_layer(x, i, mix_norm[l], gla_w_in, gla_w_g2[i], gla_b_g[i], gla_norm[i],
                                     gla_w_out, l, ffn_w_up, ffn_w_down)
        x = _ffn_layer(x, l, ffn_norm[l], ffn_w_up, ffn_conv_w[l], ffn_conv_b[l], ffn_w_down,
                       final_norm, final_norm=(l == depth - 1), precast=precast)
    return x
```
